```python
import jax, jax.numpy as jnp
from jax import lax
import numpy as np

D_MODEL = 2048
BATCH = 1
SEQ = 16384
DEPTH = 2

CHUNK = 64
LEFT_CHUNKS = 8
BAND = (LEFT_CHUNKS + 1) * CHUNK
MIX_WIDTH = D_MODEL
WIDTH_A = MIX_WIDTH // 2
N_HEADS_A = 8
HEAD_DIM_A = WIDTH_A // N_HEADS_A
REL_CLIP = 256
WIDTH_B = MIX_WIDTH - WIDTH_A
N_GROUPS_B = 8
GROUP_DIM_B = WIDTH_B // N_GROUPS_B
GMLP_BLOCK = 128
IN_PROJ = 3 * WIDTH_A + 2 * WIDTH_B
N_MEM = 256
N_HEADS_MEM = 4
HEAD_DIM_MEM = D_MODEL // N_HEADS_MEM
D_FF = ((8 * D_MODEL // 3 + 127) // 128) * 128
EPS = 1e-6
NEG = -1e30

kernel_name = "chunk_causal_hybrid_attn_gmlp_macaron"


def rmsnorm(x, g):
    xf = x.astype(jnp.float32)
    y = xf * lax.rsqrt(jnp.mean(xf * xf, axis=-1, keepdims=True) + EPS)
    return (y * g.astype(jnp.float32)).astype(x.dtype)


def layernorm(x, g, b):
    xf = x.astype(jnp.float32)
    mu = jnp.mean(xf, axis=-1, keepdims=True)
    xc = xf - mu
    y = xc * lax.rsqrt(jnp.mean(xc * xc, axis=-1, keepdims=True) + EPS)
    return (y * g.astype(jnp.float32) + b.astype(jnp.float32)).astype(x.dtype)


def swiglu(h, w_in, w_out):
    a, b = jnp.split(h @ w_in, 2, axis=-1)
    return (jax.nn.silu(a) * b) @ w_out


def rel_bias_band(table):
    s = jnp.arange(CHUNK)[:, None]
    j = jnp.arange(BAND)[None, :]
    dist = LEFT_CHUNKS * CHUNK + s - j
    idx = jnp.clip(dist, -REL_CLIP, REL_CLIP) + REL_CLIP
    return table[:, idx]


def chunk_attention(q, k, v, g_q, g_k, rel_table):
    b, s, h, dh = q.shape
    nc = s // CHUNK
    q = rmsnorm(q, g_q)
    k = rmsnorm(k, g_k)
    qc = q.reshape(b, nc, CHUNK, h, dh)
    pad = ((0, 0), (LEFT_CHUNKS, 0), (0, 0), (0, 0), (0, 0))
    kp = jnp.pad(k.reshape(b, nc, CHUNK, h, dh), pad)
    vp = jnp.pad(v.reshape(b, nc, CHUNK, h, dh), pad)
    band_idx = jnp.arange(nc)[:, None] + jnp.arange(LEFT_CHUNKS + 1)[None, :]
    kb = kp[:, band_idx].reshape(b, nc, BAND, h, dh)
    vb = vp[:, band_idx].reshape(b, nc, BAND, h, dh)
    scores = jnp.einsum('bcqhd,bckhd->bchqk', qc, kb).astype(jnp.float32) * (dh ** -0.5)
    scores = scores + rel_bias_band(rel_table).astype(jnp.float32)[None, None]
    valid = jnp.repeat(band_idx >= LEFT_CHUNKS, CHUNK, axis=1)
    scores = jnp.where(valid[None, :, None, None, :], scores, NEG)
    p = jax.nn.softmax(scores, axis=-1).astype(v.dtype)
    o = jnp.einsum('bchqk,bckhd->bcqhd', p, vb)
    return o.reshape(b, s, h * dh)


def gmlp_spatial_gate(u, v, ln_g, ln_b, w_s, b_s):
    b, s, _ = v.shape
    nb = s // GMLP_BLOCK
    v = layernorm(v, ln_g, ln_b)
    vb = v.reshape(b, nb, GMLP_BLOCK, N_GROUPS_B, GROUP_DIM_B)
    pos = jnp.arange(GMLP_BLOCK)
    mask = (pos[:, None] // CHUNK) >= (pos[None, :] // CHUNK)
    ws = jnp.where(mask[None], w_s, jnp.zeros_like(w_s))
    mixed = jnp.einsum('gst,bntgc->bnsgc', ws, vb) + b_s.T[None, None, :, :, None]
    return u * mixed.reshape(b, s, WIDTH_B)


def memory_attention(h, m, w_q, w_kv, w_o, g_q, g_k):
    b, s, _ = h.shape
    nm = m.shape[1]
    q = rmsnorm((h @ w_q).reshape(b, s, N_HEADS_MEM, HEAD_DIM_MEM), g_q)
    k, v = jnp.split(m @ w_kv, 2, axis=-1)
    k = rmsnorm(k.reshape(b, nm, N_HEADS_MEM, HEAD_DIM_MEM), g_k)
    v = v.reshape(b, nm, N_HEADS_MEM, HEAD_DIM_MEM)
    scores = jnp.einsum('bshd,bmhd->bhsm', q, k).astype(jnp.float32) * (HEAD_DIM_MEM ** -0.5)
    p = jax.nn.softmax(scores, axis=-1).astype(v.dtype)
    o = jnp.einsum('bhsm,bmhd->bshd', p, v)
    return o.reshape(b, s, D_MODEL) @ w_o


def setup_inputs(seed: int = 0) -> dict:
    key = jax.random.key(seed)
    ks = jax.random.split(key, 32)
    L = DEPTH

    def w(k, shape, fan_in):
        return jax.random.normal(k, shape, jnp.float32) * (fan_in ** -0.5)

    def gain(k, shape):
        return 1.0 + 0.02 * jax.random.normal(k, shape, jnp.float32)

    return {
        "x": jax.random.normal(ks[0], (BATCH, SEQ, D_MODEL), jnp.float32),
        "mem": jax.random.normal(ks[1], (BATCH, N_MEM, D_MODEL), jnp.float32),
        "g_ffn1": gain(ks[2], (L, D_MODEL)),
        "w_ffn1_in": w(ks[3], (L, D_MODEL, 2 * D_FF), D_MODEL),
        "w_ffn1_out": w(ks[4], (L, D_FF, D_MODEL), D_FF),
        "g_mix": gain(ks[5], (L, D_MODEL)),
        "w_in": w(ks[6], (L, D_MODEL, IN_PROJ), D_MODEL),
        "g_q_a": gain(ks[7], (L, HEAD_DIM_A)),
        "g_k_a": gain(ks[8], (L, HEAD_DIM_A)),
        "rel_table": 0.1 * jax.random.normal(ks[9], (L, N_HEADS_A, 2 * REL_CLIP + 1), jnp.float32),
        "ln_v_g": gain(ks[10], (L, WIDTH_B)),
        "ln_v_b": 0.02 * jax.random.normal(ks[11], (L, WIDTH_B), jnp.float32),
        "w_s": w(ks[12], (L, N_GROUPS_B, GMLP_BLOCK, GMLP_BLOCK), GMLP_BLOCK),
        "b_s": gain(ks[13], (L, N_GROUPS_B, GMLP_BLOCK)),
        "g_out_a": gain(ks[14], (L, WIDTH_A)),
        "g_out_b": gain(ks[15], (L, WIDTH_B)),
        "w_out": w(ks[16], (L, MIX_WIDTH, D_MODEL), MIX_WIDTH),
        "g_mem_q": gain(ks[17], (L, D_MODEL)),
        "g_mem_kv": gain(ks[18], (L, D_MODEL)),
        "w_mem_q": w(ks[19], (L, D_MODEL, D_MODEL), D_MODEL),
        "w_mem_kv": w(ks[20], (L, D_MODEL, 2 * D_MODEL), D_MODEL),
        "w_mem_o": w(ks[21], (L, D_MODEL, D_MODEL), D_MODEL),
        "g_q_mem": gain(ks[22], (L, HEAD_DIM_MEM)),
        "g_k_mem": gain(ks[23], (L, HEAD_DIM_MEM)),
        "g_ffn2": gain(ks[24], (L, D_MODEL)),
        "w_ffn2_in": w(ks[25], (L, D_MODEL, 2 * D_FF), D_MODEL),
        "w_ffn2_out": w(ks[26], (L, D_FF, D_MODEL), D_FF),
    }


def reference(x, mem, g_ffn1, w_ffn1_in, w_ffn1_out, g_mix, w_in, g_q_a, g_k_a, rel_table,
              ln_v_g, ln_v_b, w_s, b_s, g_out_a, g_out_b, w_out, g_mem_q, g_mem_kv,
              w_mem_q, w_mem_kv, w_mem_o, g_q_mem, g_k_mem, g_ffn2, w_ffn2_in, w_ffn2_out):
    b, s, _ = x.shape
    splits = [WIDTH_A, 2 * WIDTH_A, 3 * WIDTH_A, 3 * WIDTH_A + WIDTH_B]
    for l in range(DEPTH):
        x = x + 0.5 * swiglu(rmsnorm(x, g_ffn1[l]), w_ffn1_in[l], w_ffn1_out[l])
        h = rmsnorm(x, g_mix[l])
        z = h @ w_in[l]
        qa, ka, va, zu, zv = jnp.split(z, splits, axis=-1)
        heads = (b, s, N_HEADS_A, HEAD_DIM_A)
        y_a = chunk_attention(qa.reshape(heads), ka.reshape(heads), va.reshape(heads),
                              g_q_a[l], g_k_a[l], rel_table[l])
        y_b = gmlp_spatial_gate(jax.nn.gelu(zu), jax.nn.gelu(zv), ln_v_g[l], ln_v_b[l],
                                w_s[l], b_s[l])
        y = jnp.concatenate([rmsnorm(y_a, g_out_a[l]), rmsnorm(y_b, g_out_b[l])], axis=-1)
        x = x + y @ w_out[l]
        x = x + memory_attention(rmsnorm(x, g_mem_q[l]), rmsnorm(mem, g_mem_kv[l]),
                                 w_mem_q[l], w_mem_kv[l], w_mem_o[l], g_q_mem[l], g_k_mem[l])
        x = x + 0.5 * swiglu(rmsnorm(x, g_ffn2[l]), w_ffn2_in[l], w_ffn2_out[l])
    return x
```

```python
import functools

import jax
import jax.numpy as jnp
from jax import lax
from jax.experimental import pallas as pl
from jax.experimental.pallas import tpu as pltpu

F32 = jnp.float32
BF16 = jnp.bfloat16

D_MODEL = 2048
CHUNK = 64
LEFT_CHUNKS = 8
WIDTH_A = 1024
N_HEADS_A = 8
HEAD_DIM_A = 128
REL_CLIP = 256
WIDTH_B = 1024
N_GROUPS_B = 8
GROUP_DIM_B = 128
GMLP_BLOCK = 128
N_MEM = 256
N_HEADS_MEM = 4
HEAD_DIM_MEM = 512
D_FF = 5504
EPS = 1e-6
NEG = -1e30

V7X_VMEM_BYTES = 64 * 1024 * 1024
VMEM_RESERVE_BYTES = 4 * 1024 * 1024

FF_TILE = 512
D_FF_PAD = -(-D_FF // FF_TILE) * FF_TILE
TM_FFN = 512
TM_PROJ = 512
TQ = 256
KV_BLOCKS = 3
TM_MEM = 512
PROJ_GROUP = 1024


def _vmem_limit(nbytes):
    return int(min(nbytes, V7X_VMEM_BYTES - VMEM_RESERVE_BYTES))


def _rms(x, g):
    ms = jnp.mean(x * x, axis=-1, keepdims=True)
    return x * lax.rsqrt(ms + EPS) * g


def _gelu_tanh(x):
    c = 0.7978845608028654
    return 0.5 * x * (1.0 + jnp.tanh(c * (x + 0.044715 * (x * x * x))))


def _ffn_kernel(x_ref, g_ref, wa_ref, wb_ref, wo_ref, o_ref, h_ref):
    j = pl.program_id(1)

    @pl.when(j == 0)
    def _():
        x = x_ref[...]
        h_ref[...] = _rms(x, g_ref[...]).astype(BF16)
        o_ref[...] = x

    h = h_ref[...]
    a = jnp.dot(h, wa_ref[...], preferred_element_type=F32)
    b = jnp.dot(h, wb_ref[...], preferred_element_type=F32)
    t = (0.5 * a * jax.nn.sigmoid(a) * b).astype(BF16)
    o_ref[...] += jnp.dot(t, wo_ref[...], preferred_element_type=F32)


def _ffn(x, g, w_in_pad, w_out_pad):
    s, d = x.shape
    nf = D_FF_PAD // FF_TILE
    est = (2 * 2 * TM_FFN * d * 4 + TM_FFN * d * 2 + 2 * 3 * d * FF_TILE * 2
           + 4 * TM_FFN * FF_TILE * 4)
    return pl.pallas_call(
        _ffn_kernel,
        grid=(s // TM_FFN, nf),
        in_specs=[
            pl.BlockSpec((TM_FFN, d), lambda i, j: (i, 0)),
            pl.BlockSpec((1, d), lambda i, j: (0, 0)),
            pl.BlockSpec((d, FF_TILE), lambda i, j: (0, j)),
            pl.BlockSpec((d, FF_TILE), lambda i, j: (0, j + nf)),
            pl.BlockSpec((FF_TILE, d), lambda i, j: (j, 0)),
        ],
        out_specs=pl.BlockSpec((TM_FFN, d), lambda i, j: (i, 0)),
        out_shape=jax.ShapeDtypeStruct((s, d), F32),
        scratch_shapes=[pltpu.VMEM((TM_FFN, d), BF16)],
        compiler_params=pltpu.CompilerParams(
            dimension_semantics=("arbitrary", "arbitrary"),
            vmem_limit_bytes=_vmem_limit(est + 8 * 1024 * 1024)),
        name="ffn",
    )(x, g, w_in_pad, w_in_pad, w_out_pad)


def _in_proj_kernel(x_ref, g_ref, w_ref, gq_ref, gk_ref, lng_ref, lnb_ref, o_ref, h_ref):
    j = pl.program_id(1)

    @pl.when(j == 0)
    def _():
        h_ref[...] = _rms(x_ref[...], g_ref[...]).astype(BF16)

    z = jnp.dot(h_ref[...], w_ref[...], preferred_element_type=F32)

    def head_norm(gain_ref):
        for hd in range(N_HEADS_A):
            sl = slice(hd * HEAD_DIM_A, (hd + 1) * HEAD_DIM_A)
            o_ref[:, sl] = _rms(z[:, sl], gain_ref[:, sl]).astype(BF16)

    @pl.when(j == 0)
    def _():
        head_norm(gq_ref)

    @pl.when(j == 1)
    def _():
        head_norm(gk_ref)

    @pl.when(j == 2)
    def _():
        o_ref[...] = z.astype(BF16)

    @pl.when(j == 3)
    def _():
        o_ref[...] = _gelu_tanh(z).astype(BF16)

    @pl.when(j == 4)
    def _():
        v = _gelu_tanh(z)
        mu = jnp.mean(v, axis=-1, keepdims=True)
        vc = v - mu
        var = jnp.mean(vc * vc, axis=-1, keepdims=True)
        o_ref[...] = (vc * lax.rsqrt(var + EPS) * lng_ref[...] + lnb_ref[...]).astype(BF16)


def _in_proj(x, g, w_in, gq, gk, ln_g, ln_b):
    s, d = x.shape
    n_groups = w_in.shape[1] // PROJ_GROUP
    vec = lambda n: pl.BlockSpec((1, n), lambda i, j: (0, 0))
    est = (2 * TM_PROJ * d * 4 + TM_PROJ * d * 2 + 2 * d * PROJ_GROUP * 2
           + 2 * TM_PROJ * PROJ_GROUP * 2 + 4 * TM_PROJ * PROJ_GROUP * 4)
    return pl.pallas_call(
        _in_proj_kernel,
        grid=(s // TM_PROJ, n_groups),
        in_specs=[
            pl.BlockSpec((TM_PROJ, d), lambda i, j: (i, 0)),
            vec(d),
            pl.BlockSpec((d, PROJ_GROUP), lambda i, j: (0, j)),
            vec(WIDTH_A), vec(WIDTH_A), vec(WIDTH_B), vec(WIDTH_B),
        ],
        out_specs=pl.BlockSpec((TM_PROJ, PROJ_GROUP), lambda i, j: (i, j)),
        out_shape=jax.ShapeDtypeStruct((s, w_in.shape[1]), BF16),
        scratch_shapes=[pltpu.VMEM((TM_PROJ, d), BF16)],
        compiler_params=pltpu.CompilerParams(
            dimension_semantics=("arbitrary", "arbitrary"),
            vmem_limit_bytes=_vmem_limit(est + 8 * 1024 * 1024)),
        name="in_proj",
    )(x, g, w_in, gq, gk, ln_g, ln_b)


def _mix_kernel(x_ref, q_ref, k0_ref, k1_ref, k2_ref, v0_ref, v1_ref, v2_ref, u_ref, vg_ref,
                bias_ref, ws_ref, bs_ref, ga_ref, gb_ref, wo_ref, o_ref, ya_ref, yb_ref):
    i = pl.program_id(0)
    k_refs = (k0_ref, k1_ref, k2_ref)
    v_refs = (v0_ref, v1_ref, v2_ref)
    tile_bias = [jnp.where(i >= KV_BLOCKS - 1 - r, 0.0, NEG).astype(F32) for r in range(KV_BLOCKS)]

    for hd in range(N_HEADS_A):
        sl = slice(hd * HEAD_DIM_A, (hd + 1) * HEAD_DIM_A)
        q = q_ref[:, sl]
        scores = []
        for r in range(KV_BLOCKS):
            sc = lax.dot_general(q, k_refs[r][:, sl], (((1,), (1,)), ((), ())),
                                 preferred_element_type=F32)
            scores.append(sc + bias_ref[hd, :, r * TQ:(r + 1) * TQ] + tile_bias[r])
        m = scores[0].max(axis=-1, keepdims=True)
        for r in range(1, KV_BLOCKS):
            m = jnp.maximum(m, scores[r].max(axis=-1, keepdims=True))
        acc = jnp.zeros((TQ, HEAD_DIM_A), F32)
        den = jnp.zeros((TQ, 1), F32)
        for r in range(KV_BLOCKS):
            p = jnp.exp(scores[r] - m)
            den = den + p.sum(axis=-1, keepdims=True)
            acc = acc + jnp.dot(p.astype(BF16), v_refs[r][:, sl], preferred_element_type=F32)
        ya_ref[:, sl] = acc / den

    pos_s = lax.broadcasted_iota(jnp.int32, (GMLP_BLOCK, GMLP_BLOCK), 0) // CHUNK
    pos_t = lax.broadcasted_iota(jnp.int32, (GMLP_BLOCK, GMLP_BLOCK), 1) // CHUNK
    causal = pos_s >= pos_t
    for gr in range(N_GROUPS_B):
        sl = slice(gr * GROUP_DIM_B, (gr + 1) * GROUP_DIM_B)
        w = jnp.where(causal, ws_ref[gr], 0.0).astype(BF16)
        b = bs_ref[:, gr:gr + 1]
        for blk in range(TQ // GMLP_BLOCK):
            rows = slice(blk * GMLP_BLOCK, (blk + 1) * GMLP_BLOCK)
            mixed = jnp.dot(w, vg_ref[rows, sl], preferred_element_type=F32) + b
            yb_ref[rows, sl] = u_ref[rows, sl].astype(F32) * mixed

    ya = _rms(ya_ref[...], ga_ref[...]).astype(BF16)
    yb = _rms(yb_ref[...], gb_ref[...]).astype(BF16)
    y = jnp.dot(ya, wo_ref[:WIDTH_A, :], preferred_element_type=F32)
    y = y + jnp.dot(yb, wo_ref[WIDTH_A:, :], preferred_element_type=F32)
    o_ref[...] = x_ref[...] + y


def _mix(x, z, bias, w_s, b_s_t, g_out_a, g_out_b, w_out):
    s, d = x.shape
    grp = lambda c: pl.BlockSpec((TQ, PROJ_GROUP), lambda i: (i, c))
    past = lambda c, back: pl.BlockSpec((TQ, PROJ_GROUP), lambda i: (jnp.maximum(i - back, 0), c))
    const = lambda shape: pl.BlockSpec(shape, lambda i: (0,) * len(shape),
                                       pipeline_mode=pl.Buffered(1))
    est = (2 * 2 * TQ * d * 4 + 2 * 9 * TQ * PROJ_GROUP * 2 + bias.size * 4 + w_s.size * 4
           + d * d * 2 + 2 * TQ * PROJ_GROUP * 4 + 16 * TQ * TQ * 4)
    return pl.pallas_call(
        _mix_kernel,
        grid=(s // TQ,),
        in_specs=[
            pl.BlockSpec((TQ, d), lambda i: (i, 0)),
            grp(0),
            past(1, 2), past(1, 1), grp(1),
            past(2, 2), past(2, 1), grp(2),
            grp(3), grp(4),
            const(bias.shape), const(w_s.shape), const(b_s_t.shape),
            const(g_out_a.shape), const(g_out_b.shape), const(w_out.shape),
        ],
        out_specs=pl.BlockSpec((TQ, d), lambda i: (i, 0)),
        out_shape=jax.ShapeDtypeStruct((s, d), F32),
        scratch_shapes=[pltpu.VMEM((TQ, WIDTH_A), F32), pltpu.VMEM((TQ, WIDTH_B), F32)],
        compiler_params=pltpu.CompilerParams(
            dimension_semantics=("arbitrary",),
            vmem_limit_bytes=_vmem_limit(est + 8 * 1024 * 1024)),
        name="mix",
    )(x, z, z, z, z, z, z, z, z, z, bias, w_s, b_s_t, g_out_a, g_out_b, w_out)


def _rel_bias_tiles(table):
    qpos = jnp.arange(TQ)[:, None] + (KV_BLOCKS - 1) * TQ
    kpos = jnp.arange(KV_BLOCKS * TQ)[None, :]
    idx = jnp.clip(qpos - kpos, -REL_CLIP, REL_CLIP) + REL_CLIP
    dchunk = qpos // CHUNK - kpos // CHUNK
    valid = (dchunk >= 0) & (dchunk <= LEFT_CHUNKS)
    return jnp.where(valid[None], table[:, idx], NEG).astype(F32)


def _mem_kv_kernel(m_ref, g_ref, w_ref, gk_ref, o_ref):
    j = pl.program_id(0)
    h = _rms(m_ref[...], g_ref[...]).astype(BF16)
    z = jnp.dot(h, w_ref[...], preferred_element_type=F32)
    heads_per_group = PROJ_GROUP // HEAD_DIM_MEM
    n_key_groups = N_HEADS_MEM // heads_per_group

    @pl.when(j < n_key_groups)
    def _():
        for hd in range(heads_per_group):
            sl = slice(hd * HEAD_DIM_MEM, (hd + 1) * HEAD_DIM_MEM)
            o_ref[:, sl] = _rms(z[:, sl], gk_ref[...]).astype(BF16)

    @pl.when(j >= n_key_groups)
    def _():
        o_ref[...] = z.astype(BF16)


def _mem_kv(mem, g, w_kv, g_k):
    n, d = mem.shape
    n_groups = w_kv.shape[1] // PROJ_GROUP
    est = 2 * n * d * 4 + 2 * d * PROJ_GROUP * 2 + 2 * n * PROJ_GROUP * 2 + 4 * n * d * 4
    return pl.pallas_call(
        _mem_kv_kernel,
        grid=(n_groups,),
        in_specs=[
            pl.BlockSpec((n, d), lambda j: (0, 0)),
            pl.BlockSpec((1, d), lambda j: (0, 0)),
            pl.BlockSpec((d, PROJ_GROUP), lambda j: (0, j)),
            pl.BlockSpec((1, HEAD_DIM_MEM), lambda j: (0, 0)),
        ],
        out_specs=pl.BlockSpec((n, PROJ_GROUP), lambda j: (0, j)),
        out_shape=jax.ShapeDtypeStruct((n, w_kv.shape[1]), BF16),
        compiler_params=pltpu.CompilerParams(
            dimension_semantics=("arbitrary",),
            vmem_limit_bytes=_vmem_limit(est + 8 * 1024 * 1024)),
        name="mem_kv",
    )(mem, g, w_kv, g_k)


def _mem_attn_kernel(x_ref, g_ref, wq_ref, gq_ref, kv_ref, wo_ref, o_ref, att_ref):
    x = x_ref[...]
    h = _rms(x, g_ref[...]).astype(BF16)
    q = jnp.dot(h, wq_ref[...], preferred_element_type=F32)
    for hd in range(N_HEADS_MEM):
        sl = slice(hd * HEAD_DIM_MEM, (hd + 1) * HEAD_DIM_MEM)
        vsl = slice(D_MODEL + hd * HEAD_DIM_MEM, D_MODEL + (hd + 1) * HEAD_DIM_MEM)
        qh = _rms(q[:, sl], gq_ref[...]).astype(BF16)
        sc = lax.dot_general(qh, kv_ref[:, sl], (((1,), (1,)), ((), ())),
                             preferred_element_type=F32)
        p = jnp.exp(sc - sc.max(axis=-1, keepdims=True))
        den = p.sum(axis=-1, keepdims=True)
        oh = jnp.dot(p.astype(BF16), kv_ref[:, vsl], preferred_element_type=F32)
        att_ref[:, sl] = (oh / den).astype(BF16)
    o_ref[...] = x + jnp.dot(att_ref[...], wo_ref[...], preferred_element_type=F32)


def _mem_attn(x, g, w_q, g_q, kv, w_o):
    s, d = x.shape
    const = lambda shape: pl.BlockSpec(shape, lambda i: (0,) * len(shape),
                                       pipeline_mode=pl.Buffered(1))
    est = (2 * 2 * TM_MEM * d * 4 + 2 * d * d * 2 + kv.size * 2 + TM_MEM * d * 2
           + 3 * TM_MEM * d * 4)
    return pl.pallas_call(
        _mem_attn_kernel,
        grid=(s // TM_MEM,),
        in_specs=[
            pl.BlockSpec((TM_MEM, d), lambda i: (i, 0)),
            const((1, d)), const(w_q.shape), const((1, HEAD_DIM_MEM)),
            const(kv.shape), const(w_o.shape),
        ],
        out_specs=pl.BlockSpec((TM_MEM, d), lambda i: (i, 0)),
        out_shape=jax.ShapeDtypeStruct((s, d), F32),
        scratch_shapes=[pltpu.VMEM((TM_MEM, d), BF16)],
        compiler_params=pltpu.CompilerParams(
            dimension_semantics=("arbitrary",),
            vmem_limit_bytes=_vmem_limit(est + 8 * 1024 * 1024)),
        name="mem_attn",
    )(x, g, w_q, g_q, kv, w_o)


def _pad_ffn_weights(w_in, w_out):
    pad = D_FF_PAD - D_FF
    wa = jnp.pad(w_in[:, :D_FF].astype(BF16), ((0, 0), (0, pad)))
    wb = jnp.pad(w_in[:, D_FF:].astype(BF16), ((0, 0), (0, pad)))
    wo = jnp.pad(w_out.astype(BF16), ((0, pad), (0, 0)))
    return jnp.concatenate([wa, wb], axis=1), wo


def kernel(x, mem, g_ffn1, w_ffn1_in, w_ffn1_out, g_mix, w_in, g_q_a, g_k_a, rel_table, ln_v_g, ln_v_b, w_s, b_s, g_out_a, g_out_b, w_out, g_mem_q, g_mem_kv, w_mem_q, w_mem_kv, w_mem_o, g_q_mem, g_k_mem, g_ffn2, w_ffn2_in, w_ffn2_out):
    b, s, d = x.shape
    assert b == 1 and d == D_MODEL and s % TM_FFN == 0 and s % TQ == 0
    depth = g_ffn1.shape[0]
    row = lambda v: v.reshape(1, -1).astype(F32)
    xs = x.reshape(s, d)
    mem2 = mem.reshape(N_MEM, d)
    for l in range(depth):
        w1_in, w1_out = _pad_ffn_weights(w_ffn1_in[l], w_ffn1_out[l])
        xs = _ffn(xs, row(g_ffn1[l]), w1_in, w1_out)

        gq = row(jnp.tile(g_q_a[l] * (HEAD_DIM_A ** -0.5), N_HEADS_A))
        gk = row(jnp.tile(g_k_a[l], N_HEADS_A))
        z = _in_proj(xs, row(g_mix[l]), w_in[l].astype(BF16), gq, gk,
                     row(ln_v_g[l]), row(ln_v_b[l]))
        xs = _mix(xs, z, _rel_bias_tiles(rel_table[l]), w_s[l], b_s[l].T.astype(F32),
                  row(g_out_a[l]), row(g_out_b[l]), w_out[l].astype(BF16))

        kv = _mem_kv(mem2, row(g_mem_kv[l]), w_mem_kv[l].astype(BF16), row(g_k_mem[l]))
        xs = _mem_attn(xs, row(g_mem_q[l]), w_mem_q[l].astype(BF16),
                       row(g_q_mem[l] * (HEAD_DIM_MEM ** -0.5)), kv, w_mem_o[l].astype(BF16))

        w2_in, w2_out = _pad_ffn_weights(w_ffn2_in[l], w_ffn2_out[l])
        xs = _ffn(xs, row(g_ffn2[l]), w2_in, w2_out)
    return xs.reshape(b, s, d)
```

```python
import jax
import jax.numpy as jnp
from jax import lax
from jax.experimental import pallas as pl
from jax.experimental.pallas import tpu as pltpu

F32 = jnp.float32
BF16 = jnp.bfloat16

D_MODEL = 2048
CHUNK = 64
LEFT_CHUNKS = 8
WIDTH_A = 1024
N_HEADS_A = 8
HEAD_DIM_A = 128
REL_CLIP = 256
WIDTH_B = 1024
N_GROUPS_B = 8
GROUP_DIM_B = 128
GMLP_BLOCK = 128
N_MEM = 256
N_HEADS_MEM = 4
HEAD_DIM_MEM = 512
D_FF = 5504
EPS = 1e-6
NEG = -1e30

V7X_VMEM_BYTES = 64 * 1024 * 1024
VMEM_RESERVE_BYTES = 4 * 1024 * 1024
V7X_LANES = 128

FF_TILE = 512
D_FF_PAD = -(-D_FF // FF_TILE) * FF_TILE
TM_FFN = 1024
TM_PROJ = 512
PROJ_GROUP = 1024
N_PROJ_GROUPS = 5
TQ = 256
KV_BLOCKS = 3
SUB_TILES = 2
TM_MIX = SUB_TILES * TQ
BIAS_EXT = 1024
TM_MEM = 512

assert BIAS_EXT >= (KV_BLOCKS + 1) * TQ - 1 and BIAS_EXT % V7X_LANES == 0
assert KV_BLOCKS * TQ >= LEFT_CHUNKS * CHUNK + TQ


def _vmem_limit(nbytes):
    return int(min(nbytes, V7X_VMEM_BYTES - VMEM_RESERVE_BYTES))


def _rms(x, g):
    ms = jnp.mean(x * x, axis=-1, keepdims=True)
    return x * lax.rsqrt(ms + EPS) * g


def _gelu_tanh(x):
    c = 0.7978845608028654
    return 0.5 * x * (1.0 + jnp.tanh(c * (x + 0.044715 * (x * x * x))))


def _resident(shape, index):
    return pl.BlockSpec(shape, lambda *_: index, pipeline_mode=pl.Buffered(1))


def _ffn_kernel(x_ref, g_ref, wa_ref, wb_ref, wo_ref, o_ref, h_ref):
    j = pl.program_id(1)

    @pl.when(j == 0)
    def _():
        x = x_ref[...]
        h_ref[...] = _rms(x, g_ref[...]).astype(BF16)
        o_ref[...] = x

    h = h_ref[...]
    a = jnp.dot(h, wa_ref[...], preferred_element_type=F32)
    b = jnp.dot(h, wb_ref[...], preferred_element_type=F32)
    t = (0.5 * a * jax.nn.sigmoid(a) * b).astype(BF16)
    o_ref[...] += jnp.dot(t, wo_ref[...], preferred_element_type=F32)


def _ffn(x, g, wa, wb, wo, layer):
    s, d = x.shape
    nf = D_FF_PAD // FF_TILE
    est = (2 * 2 * TM_FFN * d * 4 + TM_FFN * d * 2 + 2 * 3 * d * FF_TILE * 2
           + 4 * TM_FFN * FF_TILE * 4)
    return pl.pallas_call(
        _ffn_kernel,
        grid=(s // TM_FFN, nf),
        in_specs=[
            pl.BlockSpec((TM_FFN, d), lambda i, j: (i, 0)),
            pl.BlockSpec((1, d), lambda i, j: (0, 0)),
            pl.BlockSpec((None, d, FF_TILE), lambda i, j: (layer, 0, j)),
            pl.BlockSpec((None, d, FF_TILE), lambda i, j: (layer, 0, j)),
            pl.BlockSpec((None, FF_TILE, d), lambda i, j: (layer, j, 0)),
        ],
        out_specs=pl.BlockSpec((TM_FFN, d), lambda i, j: (i, 0)),
        out_shape=jax.ShapeDtypeStruct((s, d), F32),
        scratch_shapes=[pltpu.VMEM((TM_FFN, d), BF16)],
        compiler_params=pltpu.CompilerParams(
            dimension_semantics=("arbitrary", "arbitrary"),
            vmem_limit_bytes=_vmem_limit(est + 8 * 1024 * 1024)),
        name="ffn",
    )(x, g, wa, wb, wo)


def _in_proj_kernel(x_ref, g_ref, w_ref, gq_ref, gk_ref, lng_ref, lnb_ref, o_ref):
    h = _rms(x_ref[...], g_ref[...]).astype(BF16)

    def group(c):
        cols = slice(c * PROJ_GROUP, (c + 1) * PROJ_GROUP)
        return jnp.dot(h, w_ref[:, cols], preferred_element_type=F32)

    def head_norm(c, gain_ref):
        z = group(c)
        for hd in range(N_HEADS_A):
            sl = slice(hd * HEAD_DIM_A, (hd + 1) * HEAD_DIM_A)
            o_ref[:, c * PROJ_GROUP + hd * HEAD_DIM_A:c * PROJ_GROUP + (hd + 1) * HEAD_DIM_A] = (
                _rms(z[:, sl], gain_ref[:, sl]).astype(BF16))

    head_norm(0, gq_ref)
    head_norm(1, gk_ref)
    o_ref[:, 2 * PROJ_GROUP:3 * PROJ_GROUP] = group(2).astype(BF16)
    o_ref[:, 3 * PROJ_GROUP:4 * PROJ_GROUP] = _gelu_tanh(group(3)).astype(BF16)
    v = _gelu_tanh(group(4))
    mu = jnp.mean(v, axis=-1, keepdims=True)
    vc = v - mu
    var = jnp.mean(vc * vc, axis=-1, keepdims=True)
    o_ref[:, 4 * PROJ_GROUP:] = (vc * lax.rsqrt(var + EPS) * lng_ref[...] + lnb_ref[...]).astype(BF16)


def _in_proj(x, g, w_in, layer, gq, gk, ln_g, ln_b):
    s, d = x.shape
    n = w_in.shape[2]
    vec = lambda m: pl.BlockSpec((1, m), lambda i: (0, 0))
    est = (2 * TM_PROJ * d * 4 + TM_PROJ * d * 2 + d * n * 2 + 2 * TM_PROJ * n * 2
           + 4 * TM_PROJ * PROJ_GROUP * 4)
    return pl.pallas_call(
        _in_proj_kernel,
        grid=(s // TM_PROJ,),
        in_specs=[
            pl.BlockSpec((TM_PROJ, d), lambda i: (i, 0)),
            vec(d),
            _resident((None, d, n), (layer, 0, 0)),
            vec(WIDTH_A), vec(WIDTH_A), vec(WIDTH_B), vec(WIDTH_B),
        ],
        out_specs=pl.BlockSpec((TM_PROJ, n), lambda i: (i, 0)),
        out_shape=jax.ShapeDtypeStruct((s, n), BF16),
        compiler_params=pltpu.CompilerParams(
            dimension_semantics=("arbitrary",),
            vmem_limit_bytes=_vmem_limit(est + 8 * 1024 * 1024)),
        name="in_proj",
    )(x, g, w_in, gq, gk, ln_g, ln_b)


def _mix_kernel(x_ref, q_ref, kp_ref, kc_ref, vp_ref, vc_ref, u_ref, vg_ref,
                ext_ref, ws_ref, bs_ref, ga_ref, gb_ref, wo_ref, o_ref,
                bias_ref, ya_ref, yb_ref, y_ref):
    i = pl.program_id(0)

    @pl.when(i == 0)
    def _():
        q_chunk = (lax.broadcasted_iota(jnp.int32, (TQ, KV_BLOCKS * TQ), 0)
                   + (KV_BLOCKS - 1) * TQ) // CHUNK
        k_chunk = lax.broadcasted_iota(jnp.int32, (TQ, KV_BLOCKS * TQ), 1) // CHUNK
        in_band = (q_chunk >= k_chunk) & (q_chunk - k_chunk <= LEFT_CHUNKS)
        for hd in range(N_HEADS_A):
            rows = jnp.broadcast_to(ext_ref[hd:hd + 1, :], (TQ, BIAS_EXT))
            toeplitz = pltpu.roll(rows, 0, 1, stride=1, stride_axis=0)
            bias_ref[hd] = jnp.where(in_band, toeplitz[:, :KV_BLOCKS * TQ], NEG)

    def kv_block(prev_ref, cur_ref, t, r, sl):
        ref = prev_ref if t + r < SUB_TILES else cur_ref
        start = ((t + r) % SUB_TILES) * TQ
        return ref[start:start + TQ, sl]

    pos_s = lax.broadcasted_iota(jnp.int32, (GMLP_BLOCK, GMLP_BLOCK), 0) // CHUNK
    pos_t = lax.broadcasted_iota(jnp.int32, (GMLP_BLOCK, GMLP_BLOCK), 1) // CHUNK
    causal = pos_s >= pos_t

    for t in range(SUB_TILES):
        rows = slice(t * TQ, (t + 1) * TQ)
        start_bias = [jnp.where(SUB_TILES * i + t + r >= KV_BLOCKS - 1, 0.0, NEG).astype(F32)
                      for r in range(KV_BLOCKS)]
        for hd in range(N_HEADS_A):
            sl = slice(hd * HEAD_DIM_A, (hd + 1) * HEAD_DIM_A)
            q = q_ref[rows, sl]
            scores = []
            for r in range(KV_BLOCKS):
                sc = lax.dot_general(q, kv_block(kp_ref, kc_ref, t, r, sl),
                                     (((1,), (1,)), ((), ())), preferred_element_type=F32)
                scores.append(sc + bias_ref[hd, :, r * TQ:(r + 1) * TQ] + start_bias[r])
            m = scores[0].max(axis=-1, keepdims=True)
            for r in range(1, KV_BLOCKS):
                m = jnp.maximum(m, scores[r].max(axis=-1, keepdims=True))
            acc = jnp.zeros((TQ, HEAD_DIM_A), F32)
            den = jnp.zeros((TQ, 1), F32)
            for r in range(KV_BLOCKS):
                p = jnp.exp(scores[r] - m)
                den = den + p.sum(axis=-1, keepdims=True)
                acc = acc + jnp.dot(p.astype(BF16), kv_block(vp_ref, vc_ref, t, r, sl),
                                    preferred_element_type=F32)
            ya_ref[:, sl] = acc / den

        for gr in range(N_GROUPS_B):
            sl = slice(gr * GROUP_DIM_B, (gr + 1) * GROUP_DIM_B)
            w = jnp.where(causal, ws_ref[gr], 0.0).astype(BF16)
            b = bs_ref[:, gr:gr + 1]
            for blk in range(TQ // GMLP_BLOCK):
                blk_rows = slice(blk * GMLP_BLOCK, (blk + 1) * GMLP_BLOCK)
                src_rows = slice(t * TQ + blk * GMLP_BLOCK, t * TQ + (blk + 1) * GMLP_BLOCK)
                mixed = jnp.dot(w, vg_ref[src_rows, sl], preferred_element_type=F32) + b
                yb_ref[blk_rows, sl] = u_ref[src_rows, sl].astype(F32) * mixed

        y_ref[:, :WIDTH_A] = _rms(ya_ref[...], ga_ref[...]).astype(BF16)
        y_ref[:, WIDTH_A:] = _rms(yb_ref[...], gb_ref[...]).astype(BF16)
        o_ref[rows, :] = x_ref[rows, :] + jnp.dot(y_ref[...], wo_ref[...],
                                                  preferred_element_type=F32)


def _mix(x, z, ext, w_s, b_s_t, g_out_a, g_out_b, w_out, layer):
    s, d = x.shape
    grp = lambda c: pl.BlockSpec((TM_MIX, PROJ_GROUP), lambda i: (i, c))
    prev = lambda c: pl.BlockSpec((TM_MIX, PROJ_GROUP), lambda i: (jnp.maximum(i - 1, 0), c))
    est = (2 * 2 * TM_MIX * d * 4 + 2 * 7 * TM_MIX * PROJ_GROUP * 2 + d * d * 2
           + N_HEADS_A * TQ * KV_BLOCKS * TQ * 4 + 2 * TQ * PROJ_GROUP * 4 + TQ * d * 2
           + 16 * TQ * TQ * 4 + 2 * TQ * d * 4)
    return pl.pallas_call(
        _mix_kernel,
        grid=(s // TM_MIX,),
        in_specs=[
            pl.BlockSpec((TM_MIX, d), lambda i: (i, 0)),
            grp(0), prev(1), grp(1), prev(2), grp(2), grp(3), grp(4),
            _resident(ext.shape, (0, 0)),
            _resident((None,) + w_s.shape[1:], (layer, 0, 0, 0)),
            _resident(b_s_t.shape, (0, 0)),
            _resident(g_out_a.shape, (0, 0)), _resident(g_out_b.shape, (0, 0)),
            _resident((None, d, d), (layer, 0, 0)),
        ],
        out_specs=pl.BlockSpec((TM_MIX, d), lambda i: (i, 0)),
        out_shape=jax.ShapeDtypeStruct((s, d), F32),
        scratch_shapes=[pltpu.VMEM((N_HEADS_A, TQ, KV_BLOCKS * TQ), F32),
                        pltpu.VMEM((TQ, WIDTH_A), F32), pltpu.VMEM((TQ, WIDTH_B), F32),
                        pltpu.VMEM((TQ, d), BF16)],
        compiler_params=pltpu.CompilerParams(
            dimension_semantics=("arbitrary",),
            vmem_limit_bytes=_vmem_limit(est + 8 * 1024 * 1024)),
        name="mix",
    )(x, z, z, z, z, z, z, z, ext, w_s, b_s_t, g_out_a, g_out_b, w_out)


def _rel_bias_ext(table):
    far = table[:, 2 * REL_CLIP:]
    n_head = (KV_BLOCKS - 1) * TQ - REL_CLIP
    n_tail = BIAS_EXT - n_head - (2 * REL_CLIP + 1)
    assert n_head >= 0 and n_tail >= TQ - 1
    return jnp.concatenate([jnp.broadcast_to(far, (table.shape[0], n_head)), table[:, ::-1],
                            jnp.broadcast_to(far, (table.shape[0], n_tail))], axis=1).astype(F32)


def _mem_kv_kernel(m_ref, g_ref, w_ref, gk_ref, o_ref):
    j = pl.program_id(0)
    h = _rms(m_ref[...], g_ref[...]).astype(BF16)
    z = jnp.dot(h, w_ref[...], preferred_element_type=F32)
    heads_per_group = PROJ_GROUP // HEAD_DIM_MEM
    n_key_groups = N_HEADS_MEM // heads_per_group

    @pl.when(j < n_key_groups)
    def _():
        for hd in range(heads_per_group):
            sl = slice(hd * HEAD_DIM_MEM, (hd + 1) * HEAD_DIM_MEM)
            o_ref[:, sl] = _rms(z[:, sl], gk_ref[...]).astype(BF16)

    @pl.when(j >= n_key_groups)
    def _():
        o_ref[...] = z.astype(BF16)


def _mem_kv(mem, g, w_kv, layer, g_k):
    n, d = mem.shape
    n_groups = w_kv.shape[2] // PROJ_GROUP
    est = 2 * n * d * 4 + 2 * d * PROJ_GROUP * 2 + 2 * n * PROJ_GROUP * 2 + 4 * n * d * 4
    return pl.pallas_call(
        _mem_kv_kernel,
        grid=(n_groups,),
        in_specs=[
            pl.BlockSpec((n, d), lambda j: (0, 0)),
            pl.BlockSpec((1, d), lambda j: (0, 0)),
            pl.BlockSpec((None, d, PROJ_GROUP), lambda j: (layer, 0, j)),
            pl.BlockSpec((1, HEAD_DIM_MEM), lambda j: (0, 0)),
        ],
        out_specs=pl.BlockSpec((n, PROJ_GROUP), lambda j: (0, j)),
        out_shape=jax.ShapeDtypeStruct((n, w_kv.shape[2]), BF16),
        compiler_params=pltpu.CompilerParams(
            dimension_semantics=("arbitrary",),
            vmem_limit_bytes=_vmem_limit(est + 8 * 1024 * 1024)),
        name="mem_kv",
    )(mem, g, w_kv, g_k)


def _mem_attn_kernel(x_ref, g_ref, wq_ref, gq_ref, kv_ref, wo_ref, o_ref, att_ref):
    x = x_ref[...]
    h = _rms(x, g_ref[...]).astype(BF16)
    q = jnp.dot(h, wq_ref[...], preferred_element_type=F32)
    for hd in range(N_HEADS_MEM):
        sl = slice(hd * HEAD_DIM_MEM, (hd + 1) * HEAD_DIM_MEM)
        vsl = slice(D_MODEL + hd * HEAD_DIM_MEM, D_MODEL + (hd + 1) * HEAD_DIM_MEM)
        qh = _rms(q[:, sl], gq_ref[...]).astype(BF16)
        sc = lax.dot_general(qh, kv_ref[:, sl], (((1,), (1,)), ((), ())),
                             preferred_element_type=F32)
        p = jnp.exp(sc - sc.max(axis=-1, keepdims=True))
        den = p.sum(axis=-1, keepdims=True)
        oh = jnp.dot(p.astype(BF16), kv_ref[:, vsl], preferred_element_type=F32)
        att_ref[:, sl] = (oh / den).astype(BF16)
    o_ref[...] = x + jnp.dot(att_ref[...], wo_ref[...], preferred_element_type=F32)


def _mem_attn(x, g, w_q, g_q, kv, w_o, layer):
    s, d = x.shape
    est = (2 * 2 * TM_MEM * d * 4 + 2 * d * d * 2 + kv.size * 2 + TM_MEM * d * 2
           + 3 * TM_MEM * d * 4)
    return pl.pallas_call(
        _mem_attn_kernel,
        grid=(s // TM_MEM,),
        in_specs=[
            pl.BlockSpec((TM_MEM, d), lambda i: (i, 0)),
            _resident((1, d), (0, 0)),
            _resident((None, d, d), (layer, 0, 0)),
            _resident((1, HEAD_DIM_MEM), (0, 0)),
            _resident(kv.shape, (0, 0)),
            _resident((None, d, d), (layer, 0, 0)),
        ],
        out_specs=pl.BlockSpec((TM_MEM, d), lambda i: (i, 0)),
        out_shape=jax.ShapeDtypeStruct((s, d), F32),
        scratch_shapes=[pltpu.VMEM((TM_MEM, d), BF16)],
        compiler_params=pltpu.CompilerParams(
            dimension_semantics=("arbitrary",),
            vmem_limit_bytes=_vmem_limit(est + 8 * 1024 * 1024)),
        name="mem_attn",
    )(x, g, w_q, g_q, kv, w_o)


def _swiglu_weights(w_in, w_out):
    pad = D_FF_PAD - D_FF
    wa = jnp.pad(w_in[:, :, :D_FF].astype(BF16), ((0, 0), (0, 0), (0, pad)))
    wb = jnp.pad(w_in[:, :, D_FF:].astype(BF16), ((0, 0), (0, 0), (0, pad)))
    wo = jnp.pad(w_out.astype(BF16), ((0, 0), (0, pad), (0, 0)))
    return wa, wb, wo


def kernel(x, mem, g_ffn1, w_ffn1_in, w_ffn1_out, g_mix, w_in, g_q_a, g_k_a, rel_table, ln_v_g, ln_v_b, w_s, b_s, g_out_a, g_out_b, w_out, g_mem_q, g_mem_kv, w_mem_q, w_mem_kv, w_mem_o, g_q_mem, g_k_mem, g_ffn2, w_ffn2_in, w_ffn2_out):
    b, s, d = x.shape
    assert b == 1 and d == D_MODEL and s % TM_FFN == 0 and s % TM_MIX == 0
    assert w_in.shape[2] == N_PROJ_GROUPS * PROJ_GROUP
    depth = g_ffn1.shape[0]
    row = lambda v: v.reshape(1, -1).astype(F32)
    ffn1 = _swiglu_weights(w_ffn1_in, w_ffn1_out)
    ffn2 = _swiglu_weights(w_ffn2_in, w_ffn2_out)
    w_in_b, w_out_b = w_in.astype(BF16), w_out.astype(BF16)
    w_q_b, w_kv_b, w_o_b = w_mem_q.astype(BF16), w_mem_kv.astype(BF16), w_mem_o.astype(BF16)
    xs = x.reshape(s, d)
    mem2 = mem.reshape(N_MEM, d)
    for l in range(depth):
        xs = _ffn(xs, row(g_ffn1[l]), *ffn1, l)

        gq = row(jnp.tile(g_q_a[l] * (HEAD_DIM_A ** -0.5), N_HEADS_A))
        gk = row(jnp.tile(g_k_a[l], N_HEADS_A))
        z = _in_proj(xs, row(g_mix[l]), w_in_b, l, gq, gk, row(ln_v_g[l]), row(ln_v_b[l]))
        xs = _mix(xs, z, _rel_bias_ext(rel_table[l]), w_s, b_s[l].T.astype(F32),
                  row(g_out_a[l]), row(g_out_b[l]), w_out_b, l)

        kv = _mem_kv(mem2, row(g_mem_kv[l]), w_kv_b, l, row(g_k_mem[l]))
        xs = _mem_attn(xs, row(g_mem_q[l]), w_q_b, row(g_q_mem[l] * (HEAD_DIM_MEM ** -0.5)),
                       kv, w_o_b, l)

        xs = _ffn(xs, row(g_ffn2[l]), *ffn2, l)
    return xs.reshape(b, s, d)
```

```python
import jax
import jax.numpy as jnp
from jax import lax
from jax.experimental import pallas as pl
from jax.experimental.pallas import tpu as pltpu

F32 = jnp.float32
BF16 = jnp.bfloat16

D_MODEL = 2048
CHUNK = 64
LEFT_CHUNKS = 8
WIDTH_A = 1024
N_HEADS_A = 8
HEAD_DIM_A = 128
REL_CLIP = 256
WIDTH_B = 1024
N_GROUPS_B = 8
GROUP_DIM_B = 128
GMLP_BLOCK = 128
N_MEM = 256
N_HEADS_MEM = 4
HEAD_DIM_MEM = 512
D_FF = 5504
EPS = 1e-6
NEG = -1e30

V7X_VMEM_BYTES = 64 * 1024 * 1024
VMEM_RESERVE_BYTES = 4 * 1024 * 1024
V7X_LANES = 128

FF_TILE = 512
D_FF_PAD = -(-D_FF // FF_TILE) * FF_TILE
TM_FFN = 1024
TM_PROJ = 512
PROJ_GROUP = 1024
N_PROJ_GROUPS = 5
TQ = 256
KV_BLOCKS = 3
SUB_TILES = 2
TM_MIX = SUB_TILES * TQ
BIAS_EXT = 1024
TM_MEM = 512

assert BIAS_EXT >= (KV_BLOCKS + 1) * TQ - 1 and BIAS_EXT % V7X_LANES == 0
assert KV_BLOCKS * TQ >= LEFT_CHUNKS * CHUNK + TQ


def _vmem_limit(nbytes):
    return int(min(nbytes, V7X_VMEM_BYTES - VMEM_RESERVE_BYTES))


def _rms(x, g):
    ms = jnp.mean(x * x, axis=-1, keepdims=True)
    return x * lax.rsqrt(ms + EPS) * g


def _gelu_tanh(x):
    c = 0.7978845608028654
    return 0.5 * x * (1.0 + jnp.tanh(c * (x + 0.044715 * (x * x * x))))


def _resident(shape, index):
    return pl.BlockSpec(shape, lambda *_: index, pipeline_mode=pl.Buffered(1))


def _ffn_kernel(x_ref, g_ref, wa_ref, wb_ref, wo_ref, o_ref, h_ref):
    j = pl.program_id(1)

    @pl.when(j == 0)
    def _():
        x = x_ref[...]
        h_ref[...] = _rms(x, g_ref[...]).astype(BF16)
        o_ref[...] = x

    h = h_ref[...]
    a = jnp.dot(h, wa_ref[...], preferred_element_type=F32)
    b = jnp.dot(h, wb_ref[...], preferred_element_type=F32)
    t = (0.5 * a * jax.nn.sigmoid(a) * b).astype(BF16)
    o_ref[...] += jnp.dot(t, wo_ref[...], preferred_element_type=F32)


def _ffn(x, g, w_ab, wo, layer):
    s, d = x.shape
    nf = D_FF_PAD // FF_TILE
    est = (2 * 2 * TM_FFN * d * 4 + TM_FFN * d * 2 + 2 * 3 * d * FF_TILE * 2
           + 4 * TM_FFN * FF_TILE * 4)
    return pl.pallas_call(
        _ffn_kernel,
        grid=(s // TM_FFN, nf),
        in_specs=[
            pl.BlockSpec((TM_FFN, d), lambda i, j: (i, 0)),
            pl.BlockSpec((1, d), lambda i, j: (0, 0)),
            pl.BlockSpec((None, d, FF_TILE), lambda i, j: (layer, 0, j)),
            pl.BlockSpec((None, d, FF_TILE), lambda i, j: (layer, 0, j + nf)),
            pl.BlockSpec((None, FF_TILE, d), lambda i, j: (layer, j, 0)),
        ],
        out_specs=pl.BlockSpec((TM_FFN, d), lambda i, j: (i, 0)),
        out_shape=jax.ShapeDtypeStruct((s, d), F32),
        scratch_shapes=[pltpu.VMEM((TM_FFN, d), BF16)],
        compiler_params=pltpu.CompilerParams(
            dimension_semantics=("arbitrary", "arbitrary"),
            vmem_limit_bytes=_vmem_limit(est + 8 * 1024 * 1024)),
        name="ffn",
    )(x, g, w_ab, w_ab, wo)


def _in_proj_kernel(x_ref, g_ref, w_ref, gq_ref, gk_ref, lng_ref, lnb_ref, o_ref):
    h = _rms(x_ref[...], g_ref[...]).astype(BF16)

    def group(c):
        cols = slice(c * PROJ_GROUP, (c + 1) * PROJ_GROUP)
        return jnp.dot(h, w_ref[:, cols], preferred_element_type=F32)

    def head_norm(c, gain_ref):
        z = group(c)
        for hd in range(N_HEADS_A):
            sl = slice(hd * HEAD_DIM_A, (hd + 1) * HEAD_DIM_A)
            o_ref[:, c * PROJ_GROUP + hd * HEAD_DIM_A:c * PROJ_GROUP + (hd + 1) * HEAD_DIM_A] = (
                _rms(z[:, sl], gain_ref[:, sl]).astype(BF16))

    head_norm(0, gq_ref)
    head_norm(1, gk_ref)
    o_ref[:, 2 * PROJ_GROUP:3 * PROJ_GROUP] = group(2).astype(BF16)
    o_ref[:, 3 * PROJ_GROUP:4 * PROJ_GROUP] = _gelu_tanh(group(3)).astype(BF16)
    v = _gelu_tanh(group(4))
    mu = jnp.mean(v, axis=-1, keepdims=True)
    vc = v - mu
    var = jnp.mean(vc * vc, axis=-1, keepdims=True)
    o_ref[:, 4 * PROJ_GROUP:] = (vc * lax.rsqrt(var + EPS) * lng_ref[...] + lnb_ref[...]).astype(BF16)


def _in_proj(x, g, w_in, layer, gq, gk, ln_g, ln_b):
    s, d = x.shape
    n = w_in.shape[2]
    vec = lambda m: pl.BlockSpec((1, m), lambda i: (0, 0))
    est = (2 * TM_PROJ * d * 4 + TM_PROJ * d * 2 + d * n * 2 + 2 * TM_PROJ * n * 2
           + 4 * TM_PROJ * PROJ_GROUP * 4)
    return pl.pallas_call(
        _in_proj_kernel,
        grid=(s // TM_PROJ,),
        in_specs=[
            pl.BlockSpec((TM_PROJ, d), lambda i: (i, 0)),
            vec(d),
            _resident((None, d, n), (layer, 0, 0)),
            vec(WIDTH_A), vec(WIDTH_A), vec(WIDTH_B), vec(WIDTH_B),
        ],
        out_specs=pl.BlockSpec((TM_PROJ, n), lambda i: (i, 0)),
        out_shape=jax.ShapeDtypeStruct((s, n), BF16),
        compiler_params=pltpu.CompilerParams(
            dimension_semantics=("arbitrary",),
            vmem_limit_bytes=_vmem_limit(est + 8 * 1024 * 1024)),
        name="in_proj",
    )(x, g, w_in, gq, gk, ln_g, ln_b)


def _mix_kernel(x_ref, q_ref, kp_ref, kc_ref, vp_ref, vc_ref, u_ref, vg_ref,
                ext_ref, ws_ref, bs_ref, ga_ref, gb_ref, wo_ref, o_ref,
                bias_ref, ya_ref, yb_ref, y_ref):
    i = pl.program_id(0)

    @pl.when(i == 0)
    def _():
        q_chunk = (lax.broadcasted_iota(jnp.int32, (TQ, KV_BLOCKS * TQ), 0)
                   + (KV_BLOCKS - 1) * TQ) // CHUNK
        k_chunk = lax.broadcasted_iota(jnp.int32, (TQ, KV_BLOCKS * TQ), 1) // CHUNK
        in_band = (q_chunk >= k_chunk) & (q_chunk - k_chunk <= LEFT_CHUNKS)
        for hd in range(N_HEADS_A):
            rows = jnp.broadcast_to(ext_ref[hd:hd + 1, :], (TQ, BIAS_EXT))
            toeplitz = pltpu.roll(rows, 0, 1, stride=1, stride_axis=0)
            bias_ref[hd] = jnp.where(in_band, toeplitz[:, :KV_BLOCKS * TQ], NEG)

    def kv_block(prev_ref, cur_ref, t, r, sl):
        ref = prev_ref if t + r < SUB_TILES else cur_ref
        start = ((t + r) % SUB_TILES) * TQ
        return ref[start:start + TQ, sl]

    pos_s = lax.broadcasted_iota(jnp.int32, (GMLP_BLOCK, GMLP_BLOCK), 0) // CHUNK
    pos_t = lax.broadcasted_iota(jnp.int32, (GMLP_BLOCK, GMLP_BLOCK), 1) // CHUNK
    causal = pos_s >= pos_t

    for t in range(SUB_TILES):
        rows = slice(t * TQ, (t + 1) * TQ)
        start_bias = [jnp.where(SUB_TILES * i + t + r >= KV_BLOCKS - 1, 0.0, NEG).astype(F32)
                      for r in range(KV_BLOCKS)]
        for hd in range(N_HEADS_A):
            sl = slice(hd * HEAD_DIM_A, (hd + 1) * HEAD_DIM_A)
            q = q_ref[rows, sl]
            scores = []
            for r in range(KV_BLOCKS):
                sc = lax.dot_general(q, kv_block(kp_ref, kc_ref, t, r, sl),
                                     (((1,), (1,)), ((), ())), preferred_element_type=F32)
                scores.append(sc + bias_ref[hd, :, r * TQ:(r + 1) * TQ] + start_bias[r])
            m_el = scores[0]
            for r in range(1, KV_BLOCKS):
                m_el = jnp.maximum(m_el, scores[r])
            m = m_el.max(axis=-1, keepdims=True)
            acc = jnp.zeros((TQ, HEAD_DIM_A), F32)
            p_el = jnp.zeros((TQ, TQ), F32)
            for r in range(KV_BLOCKS):
                p = jnp.exp(scores[r] - m)
                p_el = p_el + p
                acc = acc + jnp.dot(p.astype(BF16), kv_block(vp_ref, vc_ref, t, r, sl),
                                    preferred_element_type=F32)
            ya_ref[:, sl] = acc / p_el.sum(axis=-1, keepdims=True)

        for gr in range(N_GROUPS_B):
            sl = slice(gr * GROUP_DIM_B, (gr + 1) * GROUP_DIM_B)
            w = jnp.where(causal, ws_ref[gr], 0.0).astype(BF16)
            b = bs_ref[:, gr:gr + 1]
            for blk in range(TQ // GMLP_BLOCK):
                blk_rows = slice(blk * GMLP_BLOCK, (blk + 1) * GMLP_BLOCK)
                src_rows = slice(t * TQ + blk * GMLP_BLOCK, t * TQ + (blk + 1) * GMLP_BLOCK)
                mixed = jnp.dot(w, vg_ref[src_rows, sl], preferred_element_type=F32) + b
                yb_ref[blk_rows, sl] = u_ref[src_rows, sl].astype(F32) * mixed

        y_ref[:, :WIDTH_A] = _rms(ya_ref[...], ga_ref[...]).astype(BF16)
        y_ref[:, WIDTH_A:] = _rms(yb_ref[...], gb_ref[...]).astype(BF16)
        o_ref[rows, :] = x_ref[rows, :] + jnp.dot(y_ref[...], wo_ref[...],
                                                  preferred_element_type=F32)


def _mix(x, z, ext, w_s, b_s_t, g_out_a, g_out_b, w_out, layer):
    s, d = x.shape
    grp = lambda c: pl.BlockSpec((TM_MIX, PROJ_GROUP), lambda i: (i, c))
    prev = lambda c: pl.BlockSpec((TM_MIX, PROJ_GROUP), lambda i: (jnp.maximum(i - 1, 0), c))
    est = (2 * 2 * TM_MIX * d * 4 + 2 * 7 * TM_MIX * PROJ_GROUP * 2 + d * d * 2
           + N_HEADS_A * TQ * KV_BLOCKS * TQ * 4 + 2 * TQ * PROJ_GROUP * 4 + TQ * d * 2
           + 16 * TQ * TQ * 4 + 2 * TQ * d * 4)
    return pl.pallas_call(
        _mix_kernel,
        grid=(s // TM_MIX,),
        in_specs=[
            pl.BlockSpec((TM_MIX, d), lambda i: (i, 0)),
            grp(0), prev(1), grp(1), prev(2), grp(2), grp(3), grp(4),
            _resident(ext.shape, (0, 0)),
            _resident((None,) + w_s.shape[1:], (layer, 0, 0, 0)),
            _resident(b_s_t.shape, (0, 0)),
            _resident(g_out_a.shape, (0, 0)), _resident(g_out_b.shape, (0, 0)),
            _resident((None, d, d), (layer, 0, 0)),
        ],
        out_specs=pl.BlockSpec((TM_MIX, d), lambda i: (i, 0)),
        out_shape=jax.ShapeDtypeStruct((s, d), F32),
        scratch_shapes=[pltpu.VMEM((N_HEADS_A, TQ, KV_BLOCKS * TQ), F32),
                        pltpu.VMEM((TQ, WIDTH_A), F32), pltpu.VMEM((TQ, WIDTH_B), F32),
                        pltpu.VMEM((TQ, d), BF16)],
        compiler_params=pltpu.CompilerParams(
            dimension_semantics=("arbitrary",),
            vmem_limit_bytes=_vmem_limit(est + 8 * 1024 * 1024)),
        name="mix",
    )(x, z, z, z, z, z, z, z, ext, w_s, b_s_t, g_out_a, g_out_b, w_out)


def _rel_bias_ext(table):
    far = table[:, 2 * REL_CLIP:]
    n_head = (KV_BLOCKS - 1) * TQ - REL_CLIP
    n_tail = BIAS_EXT - n_head - (2 * REL_CLIP + 1)
    assert n_head >= 0 and n_tail >= TQ - 1
    return jnp.concatenate([jnp.broadcast_to(far, (table.shape[0], n_head)), table[:, ::-1],
                            jnp.broadcast_to(far, (table.shape[0], n_tail))], axis=1).astype(F32)


def _mem_kv_kernel(m_ref, g_ref, w_ref, gk_ref, o_ref):
    j = pl.program_id(0)
    h = _rms(m_ref[...], g_ref[...]).astype(BF16)
    z = jnp.dot(h, w_ref[...], preferred_element_type=F32)
    heads_per_group = PROJ_GROUP // HEAD_DIM_MEM
    n_key_groups = N_HEADS_MEM // heads_per_group

    @pl.when(j < n_key_groups)
    def _():
        for hd in range(heads_per_group):
            sl = slice(hd * HEAD_DIM_MEM, (hd + 1) * HEAD_DIM_MEM)
            o_ref[:, sl] = _rms(z[:, sl], gk_ref[...]).astype(BF16)

    @pl.when(j >= n_key_groups)
    def _():
        o_ref[...] = z.astype(BF16)


def _mem_kv(mem, g, w_kv, layer, g_k):
    n, d = mem.shape
    n_groups = w_kv.shape[2] // PROJ_GROUP
    est = 2 * n * d * 4 + 2 * d * PROJ_GROUP * 2 + 2 * n * PROJ_GROUP * 2 + 4 * n * d * 4
    return pl.pallas_call(
        _mem_kv_kernel,
        grid=(n_groups,),
        in_specs=[
            pl.BlockSpec((n, d), lambda j: (0, 0)),
            pl.BlockSpec((1, d), lambda j: (0, 0)),
            pl.BlockSpec((None, d, PROJ_GROUP), lambda j: (layer, 0, j)),
            pl.BlockSpec((1, HEAD_DIM_MEM), lambda j: (0, 0)),
        ],
        out_specs=pl.BlockSpec((n, PROJ_GROUP), lambda j: (0, j)),
        out_shape=jax.ShapeDtypeStruct((n, w_kv.shape[2]), BF16),
        compiler_params=pltpu.CompilerParams(
            dimension_semantics=("arbitrary",),
            vmem_limit_bytes=_vmem_limit(est + 8 * 1024 * 1024)),
        name="mem_kv",
    )(mem, g, w_kv, g_k)


def _mem_attn_kernel(x_ref, g_ref, wq_ref, gq_ref, kv_ref, wo_ref, o_ref, att_ref):
    x = x_ref[...]
    h = _rms(x, g_ref[...]).astype(BF16)
    q = jnp.dot(h, wq_ref[...], preferred_element_type=F32)
    for hd in range(N_HEADS_MEM):
        sl = slice(hd * HEAD_DIM_MEM, (hd + 1) * HEAD_DIM_MEM)
        vsl = slice(D_MODEL + hd * HEAD_DIM_MEM, D_MODEL + (hd + 1) * HEAD_DIM_MEM)
        qh = _rms(q[:, sl], gq_ref[...]).astype(BF16)
        sc = lax.dot_general(qh, kv_ref[:, sl], (((1,), (1,)), ((), ())),
                             preferred_element_type=F32)
        p = jnp.exp(sc - sc.max(axis=-1, keepdims=True))
        den = p.sum(axis=-1, keepdims=True)
        oh = jnp.dot(p.astype(BF16), kv_ref[:, vsl], preferred_element_type=F32)
        att_ref[:, sl] = (oh / den).astype(BF16)
    o_ref[...] = x + jnp.dot(att_ref[...], wo_ref[...], preferred_element_type=F32)


def _mem_attn(x, g, w_q, g_q, kv, w_o, layer):
    s, d = x.shape
    est = (2 * 2 * TM_MEM * d * 4 + 2 * d * d * 2 + kv.size * 2 + TM_MEM * d * 2
           + 3 * TM_MEM * d * 4)
    return pl.pallas_call(
        _mem_attn_kernel,
        grid=(s // TM_MEM,),
        in_specs=[
            pl.BlockSpec((TM_MEM, d), lambda i: (i, 0)),
            _resident((1, d), (0, 0)),
            _resident((None, d, d), (layer, 0, 0)),
            _resident((1, HEAD_DIM_MEM), (0, 0)),
            _resident(kv.shape, (0, 0)),
            _resident((None, d, d), (layer, 0, 0)),
        ],
        out_specs=pl.BlockSpec((TM_MEM, d), lambda i: (i, 0)),
        out_shape=jax.ShapeDtypeStruct((s, d), F32),
        scratch_shapes=[pltpu.VMEM((TM_MEM, d), BF16)],
        compiler_params=pltpu.CompilerParams(
            dimension_semantics=("arbitrary",),
            vmem_limit_bytes=_vmem_limit(est + 8 * 1024 * 1024)),
        name="mem_attn",
    )(x, g, w_q, g_q, kv, w_o)


def _swiglu_weights(w_in, w_out):
    depth, d, _ = w_in.shape
    pad = D_FF_PAD - D_FF
    zc = jnp.zeros((depth, d, pad), w_in.dtype)
    w_ab = jnp.concatenate([w_in[:, :, :D_FF], zc, w_in[:, :, D_FF:], zc], axis=2).astype(BF16)
    wo = jnp.concatenate([w_out, jnp.zeros((depth, pad, d), w_out.dtype)], axis=1).astype(BF16)
    return w_ab, wo


def kernel(x, mem, g_ffn1, w_ffn1_in, w_ffn1_out, g_mix, w_in, g_q_a, g_k_a, rel_table, ln_v_g, ln_v_b, w_s, b_s, g_out_a, g_out_b, w_out, g_mem_q, g_mem_kv, w_mem_q, w_mem_kv, w_mem_o, g_q_mem, g_k_mem, g_ffn2, w_ffn2_in, w_ffn2_out):
    b, s, d = x.shape
    assert b == 1 and d == D_MODEL and s % TM_FFN == 0 and s % TM_MIX == 0
    assert w_in.shape[2] == N_PROJ_GROUPS * PROJ_GROUP
    depth = g_ffn1.shape[0]
    row = lambda v: v.reshape(1, -1).astype(F32)
    ffn1 = _swiglu_weights(w_ffn1_in, w_ffn1_out)
    ffn2 = _swiglu_weights(w_ffn2_in, w_ffn2_out)
    w_in_b, w_out_b = w_in.astype(BF16), w_out.astype(BF16)
    w_q_b, w_kv_b, w_o_b = w_mem_q.astype(BF16), w_mem_kv.astype(BF16), w_mem_o.astype(BF16)
    xs = x.reshape(s, d)
    mem2 = mem.reshape(N_MEM, d)
    for l in range(depth):
        xs = _ffn(xs, row(g_ffn1[l]), *ffn1, l)

        gq = row(jnp.tile(g_q_a[l] * (HEAD_DIM_A ** -0.5), N_HEADS_A))
        gk = row(jnp.tile(g_k_a[l], N_HEADS_A))
        z = _in_proj(xs, row(g_mix[l]), w_in_b, l, gq, gk, row(ln_v_g[l]), row(ln_v_b[l]))
        xs = _mix(xs, z, _rel_bias_ext(rel_table[l]), w_s, b_s[l].T.astype(F32),
                  row(g_out_a[l]), row(g_out_b[l]), w_out_b, l)

        kv = _mem_kv(mem2, row(g_mem_kv[l]), w_kv_b, l, row(g_k_mem[l]))
        xs = _mem_attn(xs, row(g_mem_q[l]), w_q_b, row(g_q_mem[l] * (HEAD_DIM_MEM ** -0.5)),
                       kv, w_o_b, l)

        xs = _ffn(xs, row(g_ffn2[l]), *ffn2, l)
    return xs.reshape(b, s, d)
```

```python
import jax
import jax.numpy as jnp
from jax import lax
from jax.experimental import pallas as pl
from jax.experimental.pallas import tpu as pltpu

F32 = jnp.float32
BF16 = jnp.bfloat16

D_MODEL = 2048
CHUNK = 64
LEFT_CHUNKS = 8
WIDTH_A = 1024
N_HEADS_A = 8
HEAD_DIM_A = 128
REL_CLIP = 256
WIDTH_B = 1024
N_GROUPS_B = 8
GROUP_DIM_B = 128
GMLP_BLOCK = 128
N_MEM = 256
N_HEADS_MEM = 4
HEAD_DIM_MEM = 512
D_FF = 5504
EPS = 1e-6
NEG = -1e30

V7X_VMEM_BYTES = 64 * 1024 * 1024
VMEM_RESERVE_BYTES = 4 * 1024 * 1024
V7X_LANES = 128

FF_TILE = 512
N_FF_TILES = -(-D_FF // FF_TILE)
FF_OVERLAP = N_FF_TILES * FF_TILE - D_FF
TM_FFN = 1024
TM_PROJ = 512
PROJ_GROUP = 1024
N_PROJ_GROUPS = 5
TQ = 256
KV_BLOCKS = 3
SUB_TILES = 2
TM_MIX = SUB_TILES * TQ
BIAS_EXT = 1024
TM_MEM = 512

assert BIAS_EXT >= (KV_BLOCKS + 1) * TQ - 1 and BIAS_EXT % V7X_LANES == 0
assert KV_BLOCKS * TQ >= LEFT_CHUNKS * CHUNK + TQ


def _vmem_limit(nbytes):
    return int(min(nbytes, V7X_VMEM_BYTES - VMEM_RESERVE_BYTES))


def _rms(x, g):
    ms = jnp.mean(x * x, axis=-1, keepdims=True)
    return x * lax.rsqrt(ms + EPS) * g


def _gelu_tanh(x):
    c = 0.7978845608028654
    return 0.5 * x * (1.0 + jnp.tanh(c * (x + 0.044715 * (x * x * x))))


def _resident(shape, index):
    return pl.BlockSpec(shape, lambda *_: index, pipeline_mode=pl.Buffered(1))


def _ffn_kernel(x_ref, g_ref, wa_ref, wb_ref, wo_ref, o_ref, h_ref):
    j = pl.program_id(1)

    @pl.when(j == 0)
    def _():
        x = x_ref[...]
        h_ref[...] = _rms(x, g_ref[...]).astype(BF16)
        o_ref[...] = x

    h = h_ref[...]
    a = jnp.dot(h, wa_ref[0], preferred_element_type=F32)
    b = jnp.dot(h, wb_ref[0], preferred_element_type=F32)
    t = 0.5 * a * jax.nn.sigmoid(a) * b
    first_new = jnp.where(j == N_FF_TILES - 1, FF_OVERLAP, 0)
    unit = lax.broadcasted_iota(jnp.int32, t.shape, 1)
    t = jnp.where(unit >= first_new, t, 0.0).astype(BF16)
    o_ref[...] += jnp.dot(t, wo_ref[0], preferred_element_type=F32)


def _ff_tile_start(j, base=0):
    lane_blocks = jnp.minimum(j * (FF_TILE // V7X_LANES), (D_FF - FF_TILE) // V7X_LANES)
    return (lane_blocks + base // V7X_LANES) * V7X_LANES


def _ffn(x, g, w_ab, wo, layer):
    s, d = x.shape
    est = (2 * 2 * TM_FFN * d * 4 + TM_FFN * d * 2 + 2 * 3 * d * FF_TILE * 2
           + 4 * TM_FFN * FF_TILE * 4)
    return pl.pallas_call(
        _ffn_kernel,
        grid=(s // TM_FFN, N_FF_TILES),
        in_specs=[
            pl.BlockSpec((TM_FFN, d), lambda i, j: (i, 0)),
            pl.BlockSpec((1, d), lambda i, j: (0, 0)),
            pl.BlockSpec((pl.Element(1), pl.Element(d), pl.Element(FF_TILE)),
                         lambda i, j: (layer, 0, _ff_tile_start(j))),
            pl.BlockSpec((pl.Element(1), pl.Element(d), pl.Element(FF_TILE)),
                         lambda i, j: (layer, 0, _ff_tile_start(j, D_FF))),
            pl.BlockSpec((pl.Element(1), pl.Element(FF_TILE), pl.Element(d)),
                         lambda i, j: (layer, _ff_tile_start(j), 0)),
        ],
        out_specs=pl.BlockSpec((TM_FFN, d), lambda i, j: (i, 0)),
        out_shape=jax.ShapeDtypeStruct((s, d), F32),
        scratch_shapes=[pltpu.VMEM((TM_FFN, d), BF16)],
        compiler_params=pltpu.CompilerParams(
            dimension_semantics=("arbitrary", "arbitrary"),
            vmem_limit_bytes=_vmem_limit(est + 8 * 1024 * 1024)),
        name="ffn",
    )(x, g, w_ab, w_ab, wo)


def _in_proj_kernel(x_ref, g_ref, w_ref, gq_ref, gk_ref, lng_ref, lnb_ref, o_ref):
    h = _rms(x_ref[...], g_ref[...]).astype(BF16)

    def group(c):
        cols = slice(c * PROJ_GROUP, (c + 1) * PROJ_GROUP)
        return jnp.dot(h, w_ref[:, cols], preferred_element_type=F32)

    def head_norm(c, gain_ref):
        z = group(c)
        for hd in range(N_HEADS_A):
            sl = slice(hd * HEAD_DIM_A, (hd + 1) * HEAD_DIM_A)
            o_ref[:, c * PROJ_GROUP + hd * HEAD_DIM_A:c * PROJ_GROUP + (hd + 1) * HEAD_DIM_A] = (
                _rms(z[:, sl], gain_ref[:, sl]).astype(BF16))

    head_norm(0, gq_ref)
    head_norm(1, gk_ref)
    o_ref[:, 2 * PROJ_GROUP:3 * PROJ_GROUP] = group(2).astype(BF16)
    o_ref[:, 3 * PROJ_GROUP:4 * PROJ_GROUP] = _gelu_tanh(group(3)).astype(BF16)
    v = _gelu_tanh(group(4))
    mu = jnp.mean(v, axis=-1, keepdims=True)
    vc = v - mu
    var = jnp.mean(vc * vc, axis=-1, keepdims=True)
    o_ref[:, 4 * PROJ_GROUP:] = (vc * lax.rsqrt(var + EPS) * lng_ref[...] + lnb_ref[...]).astype(BF16)


def _in_proj(x, g, w_in, layer, gq, gk, ln_g, ln_b):
    s, d = x.shape
    n = w_in.shape[2]
    vec = lambda m: pl.BlockSpec((1, m), lambda i: (0, 0))
    est = (2 * TM_PROJ * d * 4 + TM_PROJ * d * 2 + d * n * 2 + 2 * TM_PROJ * n * 2
           + 4 * TM_PROJ * PROJ_GROUP * 4)
    return pl.pallas_call(
        _in_proj_kernel,
        grid=(s // TM_PROJ,),
        in_specs=[
            pl.BlockSpec((TM_PROJ, d), lambda i: (i, 0)),
            vec(d),
            _resident((None, d, n), (layer, 0, 0)),
            vec(WIDTH_A), vec(WIDTH_A), vec(WIDTH_B), vec(WIDTH_B),
        ],
        out_specs=pl.BlockSpec((TM_PROJ, n), lambda i: (i, 0)),
        out_shape=jax.ShapeDtypeStruct((s, n), BF16),
        compiler_params=pltpu.CompilerParams(
            dimension_semantics=("arbitrary",),
            vmem_limit_bytes=_vmem_limit(est + 8 * 1024 * 1024)),
        name="in_proj",
    )(x, g, w_in, gq, gk, ln_g, ln_b)


def _mix_kernel(x_ref, q_ref, kp_ref, kc_ref, vp_ref, vc_ref, u_ref, vg_ref,
                ext_ref, ws_ref, bs_ref, ga_ref, gb_ref, wo_ref, o_ref,
                bias_ref, ya_ref, yb_ref, y_ref):
    i = pl.program_id(0)

    @pl.when(i == 0)
    def _():
        q_chunk = (lax.broadcasted_iota(jnp.int32, (TQ, KV_BLOCKS * TQ), 0)
                   + (KV_BLOCKS - 1) * TQ) // CHUNK
        k_chunk = lax.broadcasted_iota(jnp.int32, (TQ, KV_BLOCKS * TQ), 1) // CHUNK
        in_band = (q_chunk >= k_chunk) & (q_chunk - k_chunk <= LEFT_CHUNKS)
        for hd in range(N_HEADS_A):
            rows = jnp.broadcast_to(ext_ref[hd:hd + 1, :], (TQ, BIAS_EXT))
            toeplitz = pltpu.roll(rows, 0, 1, stride=1, stride_axis=0)
            bias_ref[hd] = jnp.where(in_band, toeplitz[:, :KV_BLOCKS * TQ], NEG)

    def kv_block(prev_ref, cur_ref, t, r, sl):
        ref = prev_ref if t + r < SUB_TILES else cur_ref
        start = ((t + r) % SUB_TILES) * TQ
        return ref[start:start + TQ, sl]

    pos_s = lax.broadcasted_iota(jnp.int32, (GMLP_BLOCK, GMLP_BLOCK), 0) // CHUNK
    pos_t = lax.broadcasted_iota(jnp.int32, (GMLP_BLOCK, GMLP_BLOCK), 1) // CHUNK
    causal = pos_s >= pos_t

    for t in range(SUB_TILES):
        rows = slice(t * TQ, (t + 1) * TQ)
        start_bias = [jnp.where(SUB_TILES * i + t + r >= KV_BLOCKS - 1, 0.0, NEG).astype(F32)
                      for r in range(KV_BLOCKS)]
        for hd in range(N_HEADS_A):
            sl = slice(hd * HEAD_DIM_A, (hd + 1) * HEAD_DIM_A)
            q = q_ref[rows, sl]
            scores = []
            for r in range(KV_BLOCKS):
                sc = lax.dot_general(q, kv_block(kp_ref, kc_ref, t, r, sl),
                                     (((1,), (1,)), ((), ())), preferred_element_type=F32)
                scores.append(sc + bias_ref[hd, :, r * TQ:(r + 1) * TQ] + start_bias[r])
            m_el = scores[0]
            for r in range(1, KV_BLOCKS):
                m_el = jnp.maximum(m_el, scores[r])
            m = m_el.max(axis=-1, keepdims=True)
            acc = jnp.zeros((TQ, HEAD_DIM_A), F32)
            p_el = jnp.zeros((TQ, TQ), F32)
            for r in range(KV_BLOCKS):
                p = jnp.exp(scores[r] - m)
                p_el = p_el + p
                acc = acc + jnp.dot(p.astype(BF16), kv_block(vp_ref, vc_ref, t, r, sl),
                                    preferred_element_type=F32)
            ya_ref[:, sl] = acc / p_el.sum(axis=-1, keepdims=True)

        for gr in range(N_GROUPS_B):
            sl = slice(gr * GROUP_DIM_B, (gr + 1) * GROUP_DIM_B)
            w = jnp.where(causal, ws_ref[gr], 0.0).astype(BF16)
            b = bs_ref[:, gr:gr + 1]
            for blk in range(TQ // GMLP_BLOCK):
                blk_rows = slice(blk * GMLP_BLOCK, (blk + 1) * GMLP_BLOCK)
                src_rows = slice(t * TQ + blk * GMLP_BLOCK, t * TQ + (blk + 1) * GMLP_BLOCK)
                mixed = jnp.dot(w, vg_ref[src_rows, sl], preferred_element_type=F32) + b
                yb_ref[blk_rows, sl] = u_ref[src_rows, sl].astype(F32) * mixed

        y_ref[:, :WIDTH_A] = _rms(ya_ref[...], ga_ref[...]).astype(BF16)
        y_ref[:, WIDTH_A:] = _rms(yb_ref[...], gb_ref[...]).astype(BF16)
        o_ref[rows, :] = x_ref[rows, :] + jnp.dot(y_ref[...], wo_ref[...],
                                                  preferred_element_type=F32)


def _mix(x, z, ext, w_s, b_s_t, g_out_a, g_out_b, w_out, layer):
    s, d = x.shape
    grp = lambda c: pl.BlockSpec((TM_MIX, PROJ_GROUP), lambda i: (i, c))
    prev = lambda c: pl.BlockSpec((TM_MIX, PROJ_GROUP), lambda i: (jnp.maximum(i - 1, 0), c))
    est = (2 * 2 * TM_MIX * d * 4 + 2 * 7 * TM_MIX * PROJ_GROUP * 2 + d * d * 2
           + N_HEADS_A * TQ * KV_BLOCKS * TQ * 4 + 2 * TQ * PROJ_GROUP * 4 + TQ * d * 2
           + 16 * TQ * TQ * 4 + 2 * TQ * d * 4)
    return pl.pallas_call(
        _mix_kernel,
        grid=(s // TM_MIX,),
        in_specs=[
            pl.BlockSpec((TM_MIX, d), lambda i: (i, 0)),
            grp(0), prev(1), grp(1), prev(2), grp(2), grp(3), grp(4),
            _resident(ext.shape, (0, 0)),
            _resident((None,) + w_s.shape[1:], (layer, 0, 0, 0)),
            _resident(b_s_t.shape, (0, 0)),
            _resident(g_out_a.shape, (0, 0)), _resident(g_out_b.shape, (0, 0)),
            _resident((None, d, d), (layer, 0, 0)),
        ],
        out_specs=pl.BlockSpec((TM_MIX, d), lambda i: (i, 0)),
        out_shape=jax.ShapeDtypeStruct((s, d), F32),
        scratch_shapes=[pltpu.VMEM((N_HEADS_A, TQ, KV_BLOCKS * TQ), F32),
                        pltpu.VMEM((TQ, WIDTH_A), F32), pltpu.VMEM((TQ, WIDTH_B), F32),
                        pltpu.VMEM((TQ, d), BF16)],
        compiler_params=pltpu.CompilerParams(
            dimension_semantics=("arbitrary",),
            vmem_limit_bytes=_vmem_limit(est + 8 * 1024 * 1024)),
        name="mix",
    )(x, z, z, z, z, z, z, z, ext, w_s, b_s_t, g_out_a, g_out_b, w_out)


def _rel_bias_ext(table):
    far = table[:, 2 * REL_CLIP:]
    n_head = (KV_BLOCKS - 1) * TQ - REL_CLIP
    n_tail = BIAS_EXT - n_head - (2 * REL_CLIP + 1)
    assert n_head >= 0 and n_tail >= TQ - 1
    return jnp.concatenate([jnp.broadcast_to(far, (table.shape[0], n_head)), table[:, ::-1],
                            jnp.broadcast_to(far, (table.shape[0], n_tail))], axis=1).astype(F32)


def _mem_kv_kernel(m_ref, g_ref, w_ref, gk_ref, o_ref):
    j = pl.program_id(0)
    h = _rms(m_ref[...], g_ref[...]).astype(BF16)
    z = jnp.dot(h, w_ref[...], preferred_element_type=F32)
    heads_per_group = PROJ_GROUP // HEAD_DIM_MEM
    n_key_groups = N_HEADS_MEM // heads_per_group

    @pl.when(j < n_key_groups)
    def _():
        for hd in range(heads_per_group):
            sl = slice(hd * HEAD_DIM_MEM, (hd + 1) * HEAD_DIM_MEM)
            o_ref[:, sl] = _rms(z[:, sl], gk_ref[...]).astype(BF16)

    @pl.when(j >= n_key_groups)
    def _():
        o_ref[...] = z.astype(BF16)


def _mem_kv(mem, g, w_kv, layer, g_k):
    n, d = mem.shape
    n_groups = w_kv.shape[2] // PROJ_GROUP
    est = 2 * n * d * 4 + 2 * d * PROJ_GROUP * 2 + 2 * n * PROJ_GROUP * 2 + 4 * n * d * 4
    return pl.pallas_call(
        _mem_kv_kernel,
        grid=(n_groups,),
        in_specs=[
            pl.BlockSpec((n, d), lambda j: (0, 0)),
            pl.BlockSpec((1, d), lambda j: (0, 0)),
            pl.BlockSpec((None, d, PROJ_GROUP), lambda j: (layer, 0, j)),
            pl.BlockSpec((1, HEAD_DIM_MEM), lambda j: (0, 0)),
        ],
        out_specs=pl.BlockSpec((n, PROJ_GROUP), lambda j: (0, j)),
        out_shape=jax.ShapeDtypeStruct((n, w_kv.shape[2]), BF16),
        compiler_params=pltpu.CompilerParams(
            dimension_semantics=("arbitrary",),
            vmem_limit_bytes=_vmem_limit(est + 8 * 1024 * 1024)),
        name="mem_kv",
    )(mem, g, w_kv, g_k)


def _mem_attn_kernel(x_ref, g_ref, wq_ref, gq_ref, kv_ref, wo_ref, o_ref, att_ref):
    x = x_ref[...]
    h = _rms(x, g_ref[...]).astype(BF16)
    q = jnp.dot(h, wq_ref[...], preferred_element_type=F32)
    for hd in range(N_HEADS_MEM):
        sl = slice(hd * HEAD_DIM_MEM, (hd + 1) * HEAD_DIM_MEM)
        vsl = slice(D_MODEL + hd * HEAD_DIM_MEM, D_MODEL + (hd + 1) * HEAD_DIM_MEM)
        qh = _rms(q[:, sl], gq_ref[...]).astype(BF16)
        sc = lax.dot_general(qh, kv_ref[:, sl], (((1,), (1,)), ((), ())),
                             preferred_element_type=F32)
        p = jnp.exp(sc - sc.max(axis=-1, keepdims=True))
        den = p.sum(axis=-1, keepdims=True)
        oh = jnp.dot(p.astype(BF16), kv_ref[:, vsl], preferred_element_type=F32)
        att_ref[:, sl] = (oh / den).astype(BF16)
    o_ref[...] = x + jnp.dot(att_ref[...], wo_ref[...], preferred_element_type=F32)


def _mem_attn(x, g, w_q, g_q, kv, w_o, layer):
    s, d = x.shape
    est = (2 * 2 * TM_MEM * d * 4 + 2 * d * d * 2 + kv.size * 2 + TM_MEM * d * 2
           + 3 * TM_MEM * d * 4)
    return pl.pallas_call(
        _mem_attn_kernel,
        grid=(s // TM_MEM,),
        in_specs=[
            pl.BlockSpec((TM_MEM, d), lambda i: (i, 0)),
            _resident((1, d), (0, 0)),
            _resident((None, d, d), (layer, 0, 0)),
            _resident((1, HEAD_DIM_MEM), (0, 0)),
            _resident(kv.shape, (0, 0)),
            _resident((None, d, d), (layer, 0, 0)),
        ],
        out_specs=pl.BlockSpec((TM_MEM, d), lambda i: (i, 0)),
        out_shape=jax.ShapeDtypeStruct((s, d), F32),
        scratch_shapes=[pltpu.VMEM((TM_MEM, d), BF16)],
        compiler_params=pltpu.CompilerParams(
            dimension_semantics=("arbitrary",),
            vmem_limit_bytes=_vmem_limit(est + 8 * 1024 * 1024)),
        name="mem_attn",
    )(x, g, w_q, g_q, kv, w_o)


def kernel(x, mem, g_ffn1, w_ffn1_in, w_ffn1_out, g_mix, w_in, g_q_a, g_k_a, rel_table, ln_v_g, ln_v_b, w_s, b_s, g_out_a, g_out_b, w_out, g_mem_q, g_mem_kv, w_mem_q, w_mem_kv, w_mem_o, g_q_mem, g_k_mem, g_ffn2, w_ffn2_in, w_ffn2_out):
    b, s, d = x.shape
    assert b == 1 and d == D_MODEL and s % TM_FFN == 0 and s % TM_MIX == 0
    assert w_in.shape[2] == N_PROJ_GROUPS * PROJ_GROUP
    depth = g_ffn1.shape[0]
    row = lambda v: v.reshape(1, -1).astype(F32)
    assert w_ffn1_in.shape[2] == 2 * D_FF and w_ffn1_out.shape[1] == D_FF
    ffn1 = (w_ffn1_in.astype(BF16), w_ffn1_out.astype(BF16))
    ffn2 = (w_ffn2_in.astype(BF16), w_ffn2_out.astype(BF16))
    w_in_b, w_out_b = w_in.astype(BF16), w_out.astype(BF16)
    w_q_b, w_kv_b, w_o_b = w_mem_q.astype(BF16), w_mem_kv.astype(BF16), w_mem_o.astype(BF16)
    xs = x.reshape(s, d)
    mem2 = mem.reshape(N_MEM, d)
    for l in range(depth):
        xs = _ffn(xs, row(g_ffn1[l]), *ffn1, l)

        gq = row(jnp.tile(g_q_a[l] * (HEAD_DIM_A ** -0.5), N_HEADS_A))
        gk = row(jnp.tile(g_k_a[l], N_HEADS_A))
        z = _in_proj(xs, row(g_mix[l]), w_in_b, l, gq, gk, row(ln_v_g[l]), row(ln_v_b[l]))
        xs = _mix(xs, z, _rel_bias_ext(rel_table[l]), w_s, b_s[l].T.astype(F32),
                  row(g_out_a[l]), row(g_out_b[l]), w_out_b, l)

        kv = _mem_kv(mem2, row(g_mem_kv[l]), w_kv_b, l, row(g_k_mem[l]))
        xs = _mem_attn(xs, row(g_mem_q[l]), w_q_b, row(g_q_mem[l] * (HEAD_DIM_MEM ** -0.5)),
                       kv, w_o_b, l)

        xs = _ffn(xs, row(g_ffn2[l]), *ffn2, l)
    return xs.reshape(b, s, d)
```

```python
import jax
import jax.numpy as jnp
from jax import lax
from jax.experimental import pallas as pl
from jax.experimental.pallas import tpu as pltpu

F32 = jnp.float32
BF16 = jnp.bfloat16

D_MODEL = 2048
CHUNK = 64
LEFT_CHUNKS = 8
WIDTH_A = 1024
N_HEADS_A = 8
HEAD_DIM_A = 128
REL_CLIP = 256
WIDTH_B = 1024
N_GROUPS_B = 8
GROUP_DIM_B = 128
GMLP_BLOCK = 128
N_MEM = 256
N_HEADS_MEM = 4
HEAD_DIM_MEM = 512
D_FF = 5504
EPS = 1e-6
NEG = -1e30

V7X_VMEM_BYTES = 64 * 1024 * 1024
VMEM_RESERVE_BYTES = 4 * 1024 * 1024
COMPILER_TEMP_BYTES = 8 * 1024 * 1024
V7X_LANES = 128

FF_TILE = 512
N_FF_TILES = -(-D_FF // FF_TILE)
FF_OVERLAP = N_FF_TILES * FF_TILE - D_FF
TM_FFN = 1024
TM_PROJ = 512
PROJ_GROUP = 1024
N_PROJ_GROUPS = 5
TQ = 256
KV_BLOCKS = 3
SUB_TILES = 2
TM_MIX = SUB_TILES * TQ
BIAS_EXT = 1024
TM_MEM = 512

assert BIAS_EXT >= (KV_BLOCKS + 1) * TQ - 1 and BIAS_EXT % V7X_LANES == 0
assert KV_BLOCKS * TQ >= LEFT_CHUNKS * CHUNK + TQ


def _vmem_limit(block_bytes):
    return int(min(block_bytes + COMPILER_TEMP_BYTES, V7X_VMEM_BYTES - VMEM_RESERVE_BYTES))


def _rms(x, g):
    ms = jnp.mean(x * x, axis=-1, keepdims=True)
    return x * lax.rsqrt(ms + EPS) * g


def _gelu_tanh(x):
    c = 0.7978845608028654
    return 0.5 * x * (1.0 + jnp.tanh(c * (x + 0.044715 * (x * x * x))))


def _resident(shape, index):
    return pl.BlockSpec(shape, lambda *_: index, pipeline_mode=pl.Buffered(1))


def _ffn_kernel(x_ref, g_ref, wa_ref, wb_ref, wo_ref, o_ref, h_ref):
    j = pl.program_id(1)

    @pl.when(j == 0)
    def _():
        x = x_ref[...]
        h_ref[...] = _rms(x, g_ref[...]).astype(BF16)
        o_ref[...] = x

    h = h_ref[...]
    a = jnp.dot(h, wa_ref[0], preferred_element_type=F32)
    b = jnp.dot(h, wb_ref[0], preferred_element_type=F32)
    t = 0.5 * a * jax.nn.sigmoid(a) * b
    first_new = jnp.where(j == N_FF_TILES - 1, FF_OVERLAP, 0)
    unit = lax.broadcasted_iota(jnp.int32, t.shape, 1)
    t = jnp.where(unit >= first_new, t, 0.0).astype(BF16)
    o_ref[...] += jnp.dot(t, wo_ref[0], preferred_element_type=F32)


def _ff_tile_start(j, base=0):
    lane_blocks = jnp.minimum(j * (FF_TILE // V7X_LANES), (D_FF - FF_TILE) // V7X_LANES)
    return (lane_blocks + base // V7X_LANES) * V7X_LANES


def _ffn(x, g, w_ab, wo, layer):
    s, d = x.shape
    est = (2 * 2 * TM_FFN * d * 4 + TM_FFN * d * 2 + 2 * 3 * d * FF_TILE * 2
           + 4 * TM_FFN * FF_TILE * 4)
    return pl.pallas_call(
        _ffn_kernel,
        grid=(s // TM_FFN, N_FF_TILES),
        in_specs=[
            pl.BlockSpec((TM_FFN, d), lambda i, j: (i, 0)),
            pl.BlockSpec((1, d), lambda i, j: (0, 0)),
            pl.BlockSpec((pl.Element(1), pl.Element(d), pl.Element(FF_TILE)),
                         lambda i, j: (layer, 0, _ff_tile_start(j))),
            pl.BlockSpec((pl.Element(1), pl.Element(d), pl.Element(FF_TILE)),
                         lambda i, j: (layer, 0, _ff_tile_start(j, D_FF))),
            pl.BlockSpec((pl.Element(1), pl.Element(FF_TILE), pl.Element(d)),
                         lambda i, j: (layer, _ff_tile_start(j), 0)),
        ],
        out_specs=pl.BlockSpec((TM_FFN, d), lambda i, j: (i, 0)),
        out_shape=jax.ShapeDtypeStruct((s, d), F32),
        scratch_shapes=[pltpu.VMEM((TM_FFN, d), BF16)],
        compiler_params=pltpu.CompilerParams(
            dimension_semantics=("arbitrary", "arbitrary"),
            vmem_limit_bytes=_vmem_limit(est)),
        name="ffn",
    )(x, g, w_ab, w_ab, wo)


def _in_proj_kernel(x_ref, g_ref, w_ref, gq_ref, gk_ref, lng_ref, lnb_ref, o_ref):
    h = _rms(x_ref[...], g_ref[...]).astype(BF16)

    def group(c):
        cols = slice(c * PROJ_GROUP, (c + 1) * PROJ_GROUP)
        return jnp.dot(h, w_ref[:, cols], preferred_element_type=F32)

    def head_norm(c, gain_ref):
        z = group(c)
        for hd in range(N_HEADS_A):
            sl = slice(hd * HEAD_DIM_A, (hd + 1) * HEAD_DIM_A)
            o_ref[:, c * PROJ_GROUP + hd * HEAD_DIM_A:c * PROJ_GROUP + (hd + 1) * HEAD_DIM_A] = (
                _rms(z[:, sl], gain_ref[:, sl]).astype(BF16))

    head_norm(0, gq_ref)
    head_norm(1, gk_ref)
    o_ref[:, 2 * PROJ_GROUP:3 * PROJ_GROUP] = group(2).astype(BF16)
    o_ref[:, 3 * PROJ_GROUP:4 * PROJ_GROUP] = _gelu_tanh(group(3)).astype(BF16)
    v = _gelu_tanh(group(4))
    mu = jnp.mean(v, axis=-1, keepdims=True)
    vc = v - mu
    var = jnp.mean(vc * vc, axis=-1, keepdims=True)
    o_ref[:, 4 * PROJ_GROUP:] = (vc * lax.rsqrt(var + EPS) * lng_ref[...] + lnb_ref[...]).astype(BF16)


def _in_proj(x, g, w_in, layer, gq, gk, ln_g, ln_b):
    s, d = x.shape
    n = w_in.shape[2]
    vec = lambda m: pl.BlockSpec((1, m), lambda i: (0, 0))
    est = (2 * TM_PROJ * d * 4 + TM_PROJ * d * 2 + d * n * 2 + 2 * TM_PROJ * n * 2
           + 4 * TM_PROJ * PROJ_GROUP * 4)
    return pl.pallas_call(
        _in_proj_kernel,
        grid=(s // TM_PROJ,),
        in_specs=[
            pl.BlockSpec((TM_PROJ, d), lambda i: (i, 0)),
            vec(d),
            _resident((None, d, n), (layer, 0, 0)),
            vec(WIDTH_A), vec(WIDTH_A), vec(WIDTH_B), vec(WIDTH_B),
        ],
        out_specs=pl.BlockSpec((TM_PROJ, n), lambda i: (i, 0)),
        out_shape=jax.ShapeDtypeStruct((s, n), BF16),
        compiler_params=pltpu.CompilerParams(
            dimension_semantics=("arbitrary",),
            vmem_limit_bytes=_vmem_limit(est)),
        name="in_proj",
    )(x, g, w_in, gq, gk, ln_g, ln_b)


def _mix_kernel(x_ref, q_ref, kp_ref, kc_ref, vp_ref, vc_ref, u_ref, vg_ref,
                ext_ref, ws_ref, bs_ref, ga_ref, gb_ref, wo_ref, o_ref,
                bias_ref, ya_ref, yb_ref, y_ref):
    i = pl.program_id(0)

    @pl.when(i == 0)
    def _():
        q_chunk = (lax.broadcasted_iota(jnp.int32, (TQ, KV_BLOCKS * TQ), 0)
                   + (KV_BLOCKS - 1) * TQ) // CHUNK
        k_chunk = lax.broadcasted_iota(jnp.int32, (TQ, KV_BLOCKS * TQ), 1) // CHUNK
        in_band = (q_chunk >= k_chunk) & (q_chunk - k_chunk <= LEFT_CHUNKS)
        for hd in range(N_HEADS_A):
            rows = jnp.broadcast_to(ext_ref[hd:hd + 1, :], (TQ, BIAS_EXT))
            toeplitz = pltpu.roll(rows, 0, 1, stride=1, stride_axis=0)
            bias_ref[hd] = jnp.where(in_band, toeplitz[:, :KV_BLOCKS * TQ], NEG)

    def kv_block(prev_ref, cur_ref, t, r, sl):
        ref = prev_ref if t + r < SUB_TILES else cur_ref
        start = ((t + r) % SUB_TILES) * TQ
        return ref[start:start + TQ, sl]

    pos_s = lax.broadcasted_iota(jnp.int32, (GMLP_BLOCK, GMLP_BLOCK), 0) // CHUNK
    pos_t = lax.broadcasted_iota(jnp.int32, (GMLP_BLOCK, GMLP_BLOCK), 1) // CHUNK
    causal = pos_s >= pos_t

    for t in range(SUB_TILES):
        rows = slice(t * TQ, (t + 1) * TQ)
        start_bias = [jnp.where(SUB_TILES * i + t + r >= KV_BLOCKS - 1, 0.0, NEG).astype(F32)
                      for r in range(KV_BLOCKS)]
        for hd in range(N_HEADS_A):
            sl = slice(hd * HEAD_DIM_A, (hd + 1) * HEAD_DIM_A)
            q = q_ref[rows, sl]
            m = den = acc = None
            for r in reversed(range(KV_BLOCKS)):
                sc = lax.dot_general(q, kv_block(kp_ref, kc_ref, t, r, sl),
                                     (((1,), (1,)), ((), ())), preferred_element_type=F32)
                sc = sc + bias_ref[hd, :, r * TQ:(r + 1) * TQ] + start_bias[r]
                m_blk = sc.max(axis=-1, keepdims=True)
                m_new = m_blk if m is None else jnp.maximum(m, m_blk)
                p = jnp.exp(sc - m_new)
                pv = jnp.dot(p.astype(BF16), kv_block(vp_ref, vc_ref, t, r, sl),
                             preferred_element_type=F32)
                p_sum = p.sum(axis=-1, keepdims=True)
                if m is None:
                    den, acc = p_sum, pv
                else:
                    alpha = jnp.exp(m - m_new)
                    den, acc = alpha * den + p_sum, alpha * acc + pv
                m = m_new
            ya_ref[:, sl] = acc / den

        for gr in range(N_GROUPS_B):
            sl = slice(gr * GROUP_DIM_B, (gr + 1) * GROUP_DIM_B)
            w = jnp.where(causal, ws_ref[gr], 0.0).astype(BF16)
            b = bs_ref[:, gr:gr + 1]
            for blk in range(TQ // GMLP_BLOCK):
                blk_rows = slice(blk * GMLP_BLOCK, (blk + 1) * GMLP_BLOCK)
                src_rows = slice(t * TQ + blk * GMLP_BLOCK, t * TQ + (blk + 1) * GMLP_BLOCK)
                mixed = jnp.dot(w, vg_ref[src_rows, sl], preferred_element_type=F32) + b
                yb_ref[blk_rows, sl] = u_ref[src_rows, sl].astype(F32) * mixed

        y_ref[:, :WIDTH_A] = _rms(ya_ref[...], ga_ref[...]).astype(BF16)
        y_ref[:, WIDTH_A:] = _rms(yb_ref[...], gb_ref[...]).astype(BF16)
        o_ref[rows, :] = x_ref[rows, :] + jnp.dot(y_ref[...], wo_ref[...],
                                                  preferred_element_type=F32)


def _mix(x, z, ext, w_s, b_s_t, g_out_a, g_out_b, w_out, layer):
    s, d = x.shape
    grp = lambda c: pl.BlockSpec((TM_MIX, PROJ_GROUP), lambda i: (i, c))
    prev = lambda c: pl.BlockSpec((TM_MIX, PROJ_GROUP), lambda i: (jnp.maximum(i - 1, 0), c))
    est = (2 * 2 * TM_MIX * d * 4 + 2 * 7 * TM_MIX * PROJ_GROUP * 2 + d * d * 2
           + N_HEADS_A * TQ * KV_BLOCKS * TQ * 4 + 2 * TQ * PROJ_GROUP * 4 + TQ * d * 2
           + 16 * TQ * TQ * 4 + 2 * TQ * d * 4)
    return pl.pallas_call(
        _mix_kernel,
        grid=(s // TM_MIX,),
        in_specs=[
            pl.BlockSpec((TM_MIX, d), lambda i: (i, 0)),
            grp(0), prev(1), grp(1), prev(2), grp(2), grp(3), grp(4),
            _resident(ext.shape, (0, 0)),
            _resident((None,) + w_s.shape[1:], (layer, 0, 0, 0)),
            _resident(b_s_t.shape, (0, 0)),
            _resident(g_out_a.shape, (0, 0)), _resident(g_out_b.shape, (0, 0)),
            _resident((None, d, d), (layer, 0, 0)),
        ],
        out_specs=pl.BlockSpec((TM_MIX, d), lambda i: (i, 0)),
        out_shape=jax.ShapeDtypeStruct((s, d), F32),
        scratch_shapes=[pltpu.VMEM((N_HEADS_A, TQ, KV_BLOCKS * TQ), F32),
                        pltpu.VMEM((TQ, WIDTH_A), F32), pltpu.VMEM((TQ, WIDTH_B), F32),
                        pltpu.VMEM((TQ, d), BF16)],
        compiler_params=pltpu.CompilerParams(
            dimension_semantics=("arbitrary",),
            vmem_limit_bytes=_vmem_limit(est)),
        name="mix",
    )(x, z, z, z, z, z, z, z, ext, w_s, b_s_t, g_out_a, g_out_b, w_out)


def _rel_bias_ext(table):
    far = table[:, 2 * REL_CLIP:]
    n_head = (KV_BLOCKS - 1) * TQ - REL_CLIP
    n_tail = BIAS_EXT - n_head - (2 * REL_CLIP + 1)
    assert n_head >= 0 and n_tail >= TQ - 1
    return jnp.concatenate([jnp.broadcast_to(far, (table.shape[0], n_head)), table[:, ::-1],
                            jnp.broadcast_to(far, (table.shape[0], n_tail))], axis=1).astype(F32)


def _mem_kv_kernel(m_ref, g_ref, w_ref, gk_ref, o_ref):
    j = pl.program_id(0)
    h = _rms(m_ref[...], g_ref[...]).astype(BF16)
    z = jnp.dot(h, w_ref[...], preferred_element_type=F32)
    heads_per_group = PROJ_GROUP // HEAD_DIM_MEM
    n_key_groups = N_HEADS_MEM // heads_per_group

    @pl.when(j < n_key_groups)
    def _():
        for hd in range(heads_per_group):
            sl = slice(hd * HEAD_DIM_MEM, (hd + 1) * HEAD_DIM_MEM)
            o_ref[:, sl] = _rms(z[:, sl], gk_ref[...]).astype(BF16)

    @pl.when(j >= n_key_groups)
    def _():
        o_ref[...] = z.astype(BF16)


def _mem_kv(mem, g, w_kv, layer, g_k):
    n, d = mem.shape
    n_groups = w_kv.shape[2] // PROJ_GROUP
    est = 2 * n * d * 4 + 2 * d * PROJ_GROUP * 2 + 2 * n * PROJ_GROUP * 2 + 4 * n * d * 4
    return pl.pallas_call(
        _mem_kv_kernel,
        grid=(n_groups,),
        in_specs=[
            pl.BlockSpec((n, d), lambda j: (0, 0)),
            pl.BlockSpec((1, d), lambda j: (0, 0)),
            pl.BlockSpec((None, d, PROJ_GROUP), lambda j: (layer, 0, j)),
            pl.BlockSpec((1, HEAD_DIM_MEM), lambda j: (0, 0)),
        ],
        out_specs=pl.BlockSpec((n, PROJ_GROUP), lambda j: (0, j)),
        out_shape=jax.ShapeDtypeStruct((n, w_kv.shape[2]), BF16),
        compiler_params=pltpu.CompilerParams(
            dimension_semantics=("arbitrary",),
            vmem_limit_bytes=_vmem_limit(est)),
        name="mem_kv",
    )(mem, g, w_kv, g_k)


def _mem_attn_kernel(x_ref, g_ref, wq_ref, gq_ref, kv_ref, wo_ref, o_ref, att_ref):
    x = x_ref[...]
    h = _rms(x, g_ref[...]).astype(BF16)
    q = jnp.dot(h, wq_ref[...], preferred_element_type=F32)
    for hd in range(N_HEADS_MEM):
        sl = slice(hd * HEAD_DIM_MEM, (hd + 1) * HEAD_DIM_MEM)
        vsl = slice(D_MODEL + hd * HEAD_DIM_MEM, D_MODEL + (hd + 1) * HEAD_DIM_MEM)
        qh = _rms(q[:, sl], gq_ref[...]).astype(BF16)
        sc = lax.dot_general(qh, kv_ref[:, sl], (((1,), (1,)), ((), ())),
                             preferred_element_type=F32)
        p = jnp.exp(sc - sc.max(axis=-1, keepdims=True))
        den = p.sum(axis=-1, keepdims=True)
        oh = jnp.dot(p.astype(BF16), kv_ref[:, vsl], preferred_element_type=F32)
        att_ref[:, sl] = (oh / den).astype(BF16)
    o_ref[...] = x + jnp.dot(att_ref[...], wo_ref[...], preferred_element_type=F32)


def _mem_attn(x, g, w_q, g_q, kv, w_o, layer):
    s, d = x.shape
    est = (2 * 2 * TM_MEM * d * 4 + 2 * d * d * 2 + kv.size * 2 + TM_MEM * d * 2
           + 3 * TM_MEM * d * 4)
    return pl.pallas_call(
        _mem_attn_kernel,
        grid=(s // TM_MEM,),
        in_specs=[
            pl.BlockSpec((TM_MEM, d), lambda i: (i, 0)),
            _resident((1, d), (0, 0)),
            _resident((None, d, d), (layer, 0, 0)),
            _resident((1, HEAD_DIM_MEM), (0, 0)),
            _resident(kv.shape, (0, 0)),
            _resident((None, d, d), (layer, 0, 0)),
        ],
        out_specs=pl.BlockSpec((TM_MEM, d), lambda i: (i, 0)),
        out_shape=jax.ShapeDtypeStruct((s, d), F32),
        scratch_shapes=[pltpu.VMEM((TM_MEM, d), BF16)],
        compiler_params=pltpu.CompilerParams(
            dimension_semantics=("arbitrary",),
            vmem_limit_bytes=_vmem_limit(est)),
        name="mem_attn",
    )(x, g, w_q, g_q, kv, w_o)


def kernel(x, mem, g_ffn1, w_ffn1_in, w_ffn1_out, g_mix, w_in, g_q_a, g_k_a, rel_table, ln_v_g, ln_v_b, w_s, b_s, g_out_a, g_out_b, w_out, g_mem_q, g_mem_kv, w_mem_q, w_mem_kv, w_mem_o, g_q_mem, g_k_mem, g_ffn2, w_ffn2_in, w_ffn2_out):
    b, s, d = x.shape
    assert b == 1 and d == D_MODEL and s % TM_FFN == 0 and s % TM_MIX == 0
    assert w_in.shape[2] == N_PROJ_GROUPS * PROJ_GROUP
    depth = g_ffn1.shape[0]
    row = lambda v: v.reshape(1, -1).astype(F32)
    assert w_ffn1_in.shape[2] == 2 * D_FF and w_ffn1_out.shape[1] == D_FF
    ffn1 = (w_ffn1_in.astype(BF16), w_ffn1_out.astype(BF16))
    ffn2 = (w_ffn2_in.astype(BF16), w_ffn2_out.astype(BF16))
    w_in_b, w_out_b = w_in.astype(BF16), w_out.astype(BF16)
    w_q_b, w_kv_b, w_o_b = w_mem_q.astype(BF16), w_mem_kv.astype(BF16), w_mem_o.astype(BF16)
    xs = x.reshape(s, d)
    mem2 = mem.reshape(N_MEM, d)
    for l in range(depth):
        xs = _ffn(xs, row(g_ffn1[l]), *ffn1, l)

        gq = row(jnp.tile(g_q_a[l] * (HEAD_DIM_A ** -0.5), N_HEADS_A))
        gk = row(jnp.tile(g_k_a[l], N_HEADS_A))
        z = _in_proj(xs, row(g_mix[l]), w_in_b, l, gq, gk, row(ln_v_g[l]), row(ln_v_b[l]))
        xs = _mix(xs, z, _rel_bias_ext(rel_table[l]), w_s, b_s[l].T.astype(F32),
                  row(g_out_a[l]), row(g_out_b[l]), w_out_b, l)

        kv = _mem_kv(mem2, row(g_mem_kv[l]), w_kv_b, l, row(g_k_mem[l]))
        xs = _mem_attn(xs, row(g_mem_q[l]), w_q_b, row(g_q_mem[l] * (HEAD_DIM_MEM ** -0.5)),
                       kv, w_o_b, l)

        xs = _ffn(xs, row(g_ffn2[l]), *ffn2, l)
    return xs.reshape(b, s, d)
```
